```python
import jax, jax.numpy as jnp
from jax import lax
import numpy as np

D_MODEL = 1024
BATCH = 16
SEQ = 4096
DEPTH = 1

CONV_WIDTH = D_MODEL
CONV_KERNEL = 31
POOL_WIDTH = D_MODEL
POOL_WINDOWS = (2, 4, 8, 16)
N_POOL_GROUPS = 4
POOL_GROUP_DIM = POOL_WIDTH // N_POOL_GROUPS
N_BRANCHES = 2
IN_PROJ_DIM = 2 * CONV_WIDTH + POOL_WIDTH + N_BRANCHES * D_MODEL
N_EXPERTS = 256
TOP_K = 8
N_EXPERT_GROUPS = 8
TOPK_GROUPS = 4
EXPERT_DIM = 256
SHARED_DIM = 256
ROUTED_SCALE = 2.5
DISPATCH_BLOCK = 64
EPS = 1e-6

kernel_name = "hybrid_conv_pool_moe_block"


def rmsnorm(x, g):
    xf = x.astype(jnp.float32)
    y = xf * lax.rsqrt(jnp.mean(xf * xf, axis=-1, keepdims=True) + EPS)
    return (y * g.astype(jnp.float32)).astype(x.dtype)


def layernorm(x, g, b):
    xf = x.astype(jnp.float32)
    mu = jnp.mean(xf, axis=-1, keepdims=True)
    var = jnp.mean(jnp.square(xf - mu), axis=-1, keepdims=True)
    y = (xf - mu) * lax.rsqrt(var + EPS)
    return (y * g.astype(jnp.float32) + b.astype(jnp.float32)).astype(x.dtype)


def modulate(x, g_pre, shift, scale):
    return rmsnorm(x, g_pre) * (1 + scale[:, None, :]) + shift[:, None, :]


def causal_depthwise_conv(u, w, b):
    C = u.shape[-1]
    out = lax.conv_general_dilated(
        u, w[:, None, :].astype(u.dtype), window_strides=(1,),
        padding=[(w.shape[0] - 1, 0)], dimension_numbers=("NWC", "WIO", "NWC"),
        feature_group_count=C)
    return out + b


def causal_multiscale_pool(u):
    B_, S_, _ = u.shape
    uf = u.astype(jnp.float32).reshape(B_, S_, N_POOL_GROUPS, POOL_GROUP_DIM)
    outs = []
    for g, win in enumerate(POOL_WINDOWS):
        seg = uf[:, :, g, :]
        cs = jnp.cumsum(seg, axis=1)
        prev = jnp.pad(cs[:, :S_ - win], ((0, 0), (win, 0), (0, 0)))
        count = jnp.minimum(jnp.arange(1, S_ + 1), win).astype(jnp.float32)
        outs.append((cs - prev) / count[None, :, None] - seg)
    return jnp.stack(outs, axis=2).astype(u.dtype)


def token_mixer_sublayer(x, shift, scale, gate, g_pre, g_post, w_in, w_dw, b_dw,
                         ln_g, ln_b, w_conv_out, w_pool, b_pool, pool_scale, w_out):
    B_, S_, _ = x.shape
    h = modulate(x, g_pre, shift, scale)
    proj = h @ w_in
    a_val = proj[..., :CONV_WIDTH]
    a_gate = proj[..., CONV_WIDTH:2 * CONV_WIDTH]
    pool_in = proj[..., 2 * CONV_WIDTH:2 * CONV_WIDTH + POOL_WIDTH]
    gate_logits = proj[..., 2 * CONV_WIDTH + POOL_WIDTH:]
    a = a_val * jax.nn.sigmoid(a_gate)
    a = causal_depthwise_conv(a, w_dw, b_dw)
    a = jax.nn.silu(layernorm(a, ln_g, ln_b)) @ w_conv_out
    pooled = causal_multiscale_pool(pool_in)
    bm = jnp.einsum("bsgc,gcd->bsgd", pooled, w_pool).reshape(B_, S_, POOL_WIDTH)
    bm = (bm + b_pool) * pool_scale
    gates = jax.nn.sigmoid(gate_logits).reshape(B_, S_, N_BRANCHES, D_MODEL)
    mixed = gates[:, :, 0, :] * a + gates[:, :, 1, :] * bm
    y = mixed @ w_out
    return x + gate[:, None, :] * rmsnorm(y, g_post)


def route(h, w_router, router_bias):
    S_ = h.shape[0]
    scores = jax.nn.sigmoid((h @ w_router).astype(jnp.float32))
    choice = scores + router_bias.astype(jnp.float32)
    grp = choice.reshape(S_, N_EXPERT_GROUPS, N_EXPERTS // N_EXPERT_GROUPS)
    grp_score = lax.top_k(grp, 2)[0].sum(-1)
    _, top_g = lax.top_k(grp_score, TOPK_GROUPS)
    gmask = jnp.zeros((S_, N_EXPERT_GROUPS), bool).at[jnp.arange(S_)[:, None], top_g].set(True)
    emask = jnp.repeat(gmask, N_EXPERTS // N_EXPERT_GROUPS, axis=1)
    _, idx = lax.top_k(jnp.where(emask, choice, -jnp.inf), TOP_K)
    w = jnp.take_along_axis(scores, idx, axis=-1)
    w = w / jnp.sum(w, axis=-1, keepdims=True) * ROUTED_SCALE
    return idx, w


def moe_sequence(h, w_router, router_bias, w_gate_up, w_down, ws_gate_up, ws_down):
    S_, D_ = h.shape
    idx, w = route(h, w_router, router_bias)
    M = S_ * TOP_K
    NB = M // DISPATCH_BLOCK + N_EXPERTS
    flat_e = idx.reshape(-1)
    flat_tok = jnp.arange(M, dtype=jnp.int32) // TOP_K
    order = jnp.argsort(flat_e, stable=True)
    sorted_e = flat_e[order]
    sorted_tok = flat_tok[order]
    sorted_w = w.reshape(-1)[order]
    counts = jnp.zeros((N_EXPERTS,), jnp.int32).at[flat_e].add(1)
    padded = (counts + DISPATCH_BLOCK - 1) // DISPATCH_BLOCK * DISPATCH_BLOCK
    pad_end = jnp.cumsum(padded)
    pad_start = pad_end - padded
    start = jnp.cumsum(counts) - counts
    dest = pad_start[sorted_e] + (jnp.arange(M, dtype=jnp.int32) - start[sorted_e])
    buf = jnp.zeros((NB * DISPATCH_BLOCK, D_), h.dtype).at[dest].set(h[sorted_tok])
    block_e = jnp.minimum(
        jnp.searchsorted(pad_end, jnp.arange(NB, dtype=jnp.int32) * DISPATCH_BLOCK, side="right"),
        N_EXPERTS - 1)

    def expert_block(args):
        xb, e = args
        gu = xb @ w_gate_up[e]
        return (jax.nn.silu(gu[:, :EXPERT_DIM]) * gu[:, EXPERT_DIM:]) @ w_down[e]

    y_buf = lax.map(expert_block, (buf.reshape(NB, DISPATCH_BLOCK, D_), block_e))
    y = (y_buf.reshape(NB * DISPATCH_BLOCK, D_)[dest] * sorted_w[:, None]).astype(h.dtype)
    routed = jnp.zeros((S_, D_), h.dtype).at[sorted_tok].add(y)
    gs = h @ ws_gate_up
    shared = (jax.nn.silu(gs[:, :SHARED_DIM]) * gs[:, SHARED_DIM:]) @ ws_down
    return routed + shared


def moe_sublayer(x, shift, scale, gate, g_pre, g_post, w_router, router_bias,
                 w_gate_up, w_down, ws_gate_up, ws_down):
    h = modulate(x, g_pre, shift, scale)
    y = lax.map(lambda hs: moe_sequence(hs, w_router, router_bias, w_gate_up, w_down,
                                        ws_gate_up, ws_down), h)
    return x + gate[:, None, :] * rmsnorm(y, g_post)


def setup_inputs(seed: int = 0) -> dict:
    key = jax.random.key(seed)
    ks = jax.random.split(key, 26)
    f32 = jnp.float32
    L, D = DEPTH, D_MODEL

    def nrm(k, shape, scale):
        return jax.random.normal(k, shape, f32) * scale

    def gain(k, shape):
        return 1.0 + 0.1 * jax.random.normal(k, shape, f32)

    return {
        "x": nrm(ks[0], (BATCH, SEQ, D), 1.0),
        "c": nrm(ks[1], (BATCH, D), 1.0),
        "w_ada": nrm(ks[2], (L, D, 6 * D), 0.5 * D ** -0.5),
        "b_ada": nrm(ks[3], (L, 6 * D), 0.02),
        "g_pre_mix": gain(ks[4], (L, D)),
        "g_post_mix": gain(ks[5], (L, D)),
        "g_pre_ffn": gain(ks[6], (L, D)),
        "g_post_ffn": gain(ks[7], (L, D)),
        "w_in": nrm(ks[8], (L, D, IN_PROJ_DIM), D ** -0.5),
        "w_dw": nrm(ks[9], (L, CONV_KERNEL, CONV_WIDTH), CONV_KERNEL ** -0.5),
        "b_dw": nrm(ks[10], (L, CONV_WIDTH), 0.02),
        "ln_g": gain(ks[11], (L, CONV_WIDTH)),
        "ln_b": nrm(ks[12], (L, CONV_WIDTH), 0.02),
        "w_conv_out": nrm(ks[13], (L, CONV_WIDTH, D), CONV_WIDTH ** -0.5),
        "w_pool": nrm(ks[14], (L, N_POOL_GROUPS, POOL_GROUP_DIM, POOL_GROUP_DIM), POOL_GROUP_DIM ** -0.5),
        "b_pool": nrm(ks[15], (L, POOL_WIDTH), 0.02),
        "pool_scale": gain(ks[16], (L, POOL_WIDTH)),
        "w_out": nrm(ks[17], (L, D, D), D ** -0.5),
        "w_router": nrm(ks[18], (L, D, N_EXPERTS), D ** -0.5),
        "router_bias": nrm(ks[19], (L, N_EXPERTS), 0.01),
        "w_gate_up": nrm(ks[20], (L, N_EXPERTS, D, 2 * EXPERT_DIM), D ** -0.5),
        "w_down": nrm(ks[21], (L, N_EXPERTS, EXPERT_DIM, D), EXPERT_DIM ** -0.5),
        "ws_gate_up": nrm(ks[22], (L, D, 2 * SHARED_DIM), D ** -0.5),
        "ws_down": nrm(ks[23], (L, SHARED_DIM, D), SHARED_DIM ** -0.5),
    }


def reference(x, c, w_ada, b_ada, g_pre_mix, g_post_mix, g_pre_ffn, g_post_ffn, w_in, w_dw,
              b_dw, ln_g, ln_b, w_conv_out, w_pool, b_pool, pool_scale, w_out, w_router,
              router_bias, w_gate_up, w_down, ws_gate_up, ws_down):
    cond = jax.nn.silu(c)
    for l in range(DEPTH):
        mod = cond @ w_ada[l] + b_ada[l]
        shift_m, scale_m, gate_m, shift_f, scale_f, gate_f = jnp.split(mod, 6, axis=-1)
        x = token_mixer_sublayer(x, shift_m, scale_m, gate_m, g_pre_mix[l], g_post_mix[l],
                                 w_in[l], w_dw[l], b_dw[l], ln_g[l], ln_b[l], w_conv_out[l],
                                 w_pool[l], b_pool[l], pool_scale[l], w_out[l])
        x = moe_sublayer(x, shift_f, scale_f, gate_f, g_pre_ffn[l], g_post_ffn[l],
                         w_router[l], router_bias[l], w_gate_up[l], w_down[l],
                         ws_gate_up[l], ws_down[l])
    return x
```

```python
import functools

import jax
import jax.numpy as jnp
from jax import lax
from jax.experimental import pallas as pl
from jax.experimental.pallas import tpu as pltpu
from jax.experimental.pallas import tpu_sc as plsc

EPS = 1e-6
TOP_K = 8
N_EXPERT_GROUPS = 8
TOPK_GROUPS = 4
ROUTED_SCALE = 2.5
POOL_WINDOWS = (2, 4, 8, 16)

LANES = 128
SUBLANES = 8
CONV_HIST = 32
POOL_HIST = 16
SC_ROWS = 128
VMEM_LIMIT = 56 * 1024 * 1024

f32 = jnp.float32
bf16 = jnp.bfloat16
i32 = jnp.int32


def _sigmoid(v):
    return 1.0 / (1.0 + jnp.exp(-v))


def _silu(v):
    return v * _sigmoid(v)


def _pack_bf16_pair(lo, hi):
    lo_bits = lax.bitcast_convert_type(lo.astype(bf16).astype(f32), jnp.uint32)
    hi_bits = lax.bitcast_convert_type(hi.astype(bf16).astype(f32), jnp.uint32)
    word = lax.shift_right_logical(lo_bits, jnp.uint32(16)) | hi_bits
    return lax.bitcast_convert_type(word, i32)


def _unpack_bf16_pair(word):
    w = lax.bitcast_convert_type(word, jnp.uint32)
    lo = lax.bitcast_convert_type(lax.shift_left(w, jnp.uint32(16)), f32)
    hi = lax.bitcast_convert_type(w & jnp.uint32(0xFFFF0000), f32)
    return lo, hi


def _adaln_kernel(c_ref, w_ref, b_ref, o_ref):
    cond = _silu(c_ref[...])
    o_ref[...] = jnp.dot(cond.astype(bf16), w_ref[...].astype(bf16), preferred_element_type=f32) + b_ref[...]


def _adaln(c, w_ada, b_ada):
    B, D = c.shape
    N = w_ada.shape[1]
    tn = 1024
    return pl.pallas_call(
        _adaln_kernel,
        grid=(N // tn,),
        in_specs=[pl.BlockSpec((B, D), lambda j: (0, 0)),
                  pl.BlockSpec((D, tn), lambda j: (0, j)),
                  pl.BlockSpec((1, tn), lambda j: (0, j))],
        out_specs=pl.BlockSpec((B, tn), lambda j: (0, j)),
        out_shape=jax.ShapeDtypeStruct((B, N), f32),
        name="adaln",
    )(c, w_ada, b_ada.reshape(1, N))


def _mixer_kernel(x_ref, mod_ref, gpre_ref, gpost_ref, win_ref, wdw_ref, bdw_ref, lng_ref, lnb_ref,
                  wco_ref, wpool_ref, bpool_ref, pscale_ref, wout_ref, o_ref,
                  aext_ref, pext_ref, act_ref, *, ts, D, conv_k):
    s = pl.program_id(1)

    @pl.when(s == 0)
    def _():
        aext_ref[0:CONV_HIST, :] = jnp.zeros((CONV_HIST, D), f32)
        pext_ref[0:POOL_HIST, :] = jnp.zeros((POOL_HIST, D), f32)

    x = x_ref[0]
    shift, scale, gate = mod_ref[0, 0:1, :], mod_ref[0, 1:2, :], mod_ref[0, 2:3, :]
    h = x * lax.rsqrt(jnp.mean(x * x, axis=-1, keepdims=True) + EPS) * gpre_ref[...]
    hb = (h * (1.0 + scale) + shift).astype(bf16)

    a_val = jnp.dot(hb, win_ref[:, 0:D], preferred_element_type=f32)
    a_gate = jnp.dot(hb, win_ref[:, D:2 * D], preferred_element_type=f32)
    aext_ref[CONV_HIST:CONV_HIST + ts, :] = a_val * _sigmoid(a_gate)
    rc = 2 * SUBLANES
    off0 = CONV_HIST - (conv_k - 1)
    for c in range(ts // rc):
        acc = jnp.broadcast_to(bdw_ref[...], (rc, D))
        for k in range(conv_k):
            acc = acc + wdw_ref[k:k + 1, :] * aext_ref[c * rc + off0 + k:c * rc + off0 + k + rc, :]
        mu = jnp.mean(acc, axis=-1, keepdims=True)
        cen = acc - mu
        var = jnp.mean(cen * cen, axis=-1, keepdims=True)
        ln = cen * lax.rsqrt(var + EPS) * lng_ref[...] + lnb_ref[...]
        act_ref[c * rc:(c + 1) * rc, :] = _silu(ln).astype(bf16)
    aext_ref[0:CONV_HIST, :] = aext_ref[ts:ts + CONV_HIST, :]
    a_out = jnp.dot(act_ref[...], wco_ref[...], preferred_element_type=f32)

    pext_ref[POOL_HIST:POOL_HIST + ts, :] = jnp.dot(hb, win_ref[:, 2 * D:3 * D], preferred_element_type=f32)
    t_glob = s * ts + lax.broadcasted_iota(i32, (ts, 1), 0)
    dg = D // len(POOL_WINDOWS)
    bms = []
    for g, win in enumerate(POOL_WINDOWS):
        cols = slice(g * dg, (g + 1) * dg)
        cur = pext_ref[POOL_HIST:POOL_HIST + ts, cols]
        wsum = cur
        for j in range(1, win):
            wsum = wsum + pext_ref[POOL_HIST - j:POOL_HIST - j + ts, cols]
        cnt = jnp.minimum(t_glob + 1, win).astype(f32)
        pooled = wsum / cnt - cur
        bms.append(jnp.dot(pooled.astype(bf16), wpool_ref[g], preferred_element_type=f32))
    pext_ref[0:POOL_HIST, :] = pext_ref[ts:ts + POOL_HIST, :]
    bm = (jnp.concatenate(bms, axis=-1) + bpool_ref[...]) * pscale_ref[...]

    g_a = _sigmoid(jnp.dot(hb, win_ref[:, 3 * D:4 * D], preferred_element_type=f32))
    g_b = _sigmoid(jnp.dot(hb, win_ref[:, 4 * D:5 * D], preferred_element_type=f32))
    mixed = (g_a * a_out + g_b * bm).astype(bf16)
    y = jnp.dot(mixed, wout_ref[...], preferred_element_type=f32)
    yn = y * lax.rsqrt(jnp.mean(y * y, axis=-1, keepdims=True) + EPS) * gpost_ref[...]
    o_ref[0] = x + gate * yn


def _mixer(x, mod_m, g_pre, g_post, w_in, w_dw, b_dw, ln_g, ln_b, w_conv_out, w_pool, b_pool, pool_scale,
           w_out, *, ts):
    B, S, D = x.shape
    conv_k = w_dw.shape[0]
    assert conv_k - 1 <= CONV_HIST and max(POOL_WINDOWS) - 1 <= POOL_HIST
    assert S % ts == 0 and ts >= CONV_HIST and D % (LANES * len(POOL_WINDOWS)) == 0
    row = lambda v: v.reshape(1, D)
    vm = pl.BlockSpec(memory_space=pltpu.VMEM)
    kern = functools.partial(_mixer_kernel, ts=ts, D=D, conv_k=conv_k)
    return pl.pallas_call(
        kern,
        grid=(B, S // ts),
        in_specs=[pl.BlockSpec((1, ts, D), lambda b, s: (b, s, 0)),
                  pl.BlockSpec((1, 3, D), lambda b, s: (b, 0, 0)),
                  vm, vm, vm, vm, vm, vm, vm, vm, vm, vm, vm, vm],
        out_specs=pl.BlockSpec((1, ts, D), lambda b, s: (b, s, 0)),
        out_shape=jax.ShapeDtypeStruct((B, S, D), f32),
        scratch_shapes=[pltpu.VMEM((ts + CONV_HIST, D), f32),
                        pltpu.VMEM((ts + POOL_HIST, D), f32),
                        pltpu.VMEM((ts, D), bf16)],
        compiler_params=pltpu.CompilerParams(dimension_semantics=("arbitrary", "arbitrary"),
                                             vmem_limit_bytes=VMEM_LIMIT),
        name="mixer",
    )(x, mod_m, row(g_pre), row(g_post), w_in.astype(bf16), w_dw, row(b_dw), row(ln_g), row(ln_b),
      w_conv_out.astype(bf16), w_pool.astype(bf16), row(b_pool), row(pool_scale), w_out.astype(bf16))


def _route_kernel(x_ref, mod_ref, gpre_ref, wr_ref, bias_ref,
                  h2p_ref, eidx_ref, rank_ref, wcol_ref, cnt_ref, carry_ref, *, tt, D, E):
    i = pl.program_id(0)

    @pl.when(i == 0)
    def _():
        carry_ref[...] = jnp.zeros((E, LANES), f32)

    x = x_ref[...]
    shift, scale = mod_ref[0, 0:1, :], mod_ref[0, 1:2, :]
    h = x * lax.rsqrt(jnp.mean(x * x, axis=-1, keepdims=True) + EPS) * gpre_ref[...]
    h = h * (1.0 + scale) + shift
    h2p_ref[...] = _pack_bf16_pair(h[:, :D // 2], h[:, D // 2:])

    logits = lax.dot_general(wr_ref[...], h.astype(bf16), (((1,), (1,)), ((), ())),
                             preferred_element_type=f32)
    scores = _sigmoid(logits)
    choice = scores + bias_ref[...]
    G = N_EXPERT_GROUPS
    ge = E // G
    neg = jnp.float32(-jnp.inf)

    ch3 = choice.reshape(G, ge, tt)
    m1 = jnp.max(ch3, axis=1, keepdims=True)
    is_m1 = ch3 == m1
    n_m1 = jnp.sum(is_m1.astype(f32), axis=1, keepdims=True)
    m2 = jnp.max(jnp.where(is_m1, neg, ch3), axis=1, keepdims=True)
    gs = (m1 + jnp.where(n_m1 >= 2.0, m1, m2)).reshape(G, tt)

    gi = lax.broadcasted_iota(i32, (G, tt), 0).astype(f32)
    gsel = jnp.zeros((G, tt), f32)
    for _ in range(TOPK_GROUPS):
        gm = jnp.max(gs, axis=0, keepdims=True)
        first = jnp.min(jnp.where(gs == gm, gi, float(G)), axis=0, keepdims=True)
        hit = gi == first
        gsel = jnp.where(hit, 1.0, gsel)
        gs = jnp.where(hit, neg, gs)

    emask = jnp.broadcast_to(gsel.reshape(G, 1, tt), (G, ge, tt)).reshape(E, tt)
    masked = jnp.where(emask > 0.5, choice, neg)
    ei = lax.broadcasted_iota(i32, (E, tt), 0).astype(f32)
    sel = jnp.zeros((E, tt), f32)
    idx_rows, sc_rows = [], []
    for _ in range(TOP_K):
        mx = jnp.max(masked, axis=0, keepdims=True)
        first = jnp.min(jnp.where(masked == mx, ei, float(E)), axis=0, keepdims=True)
        hit = ei == first
        idx_rows.append(first)
        sc_rows.append(jnp.sum(jnp.where(hit, scores, 0.0), axis=0, keepdims=True))
        sel = sel + hit.astype(f32)
        masked = jnp.where(hit, neg, masked)

    upper = (lax.broadcasted_iota(i32, (tt, tt), 0) < lax.broadcasted_iota(i32, (tt, tt), 1)).astype(bf16)
    selb = sel.astype(bf16)
    excl = jnp.dot(selb, upper, preferred_element_type=f32)
    carry = carry_ref[...]
    pos = excl + jnp.concatenate([carry] * (tt // LANES), axis=1)
    carry = carry + jnp.dot(selb, jnp.ones((tt, LANES), bf16), preferred_element_type=f32)
    carry_ref[...] = carry
    cnt_ref[...] = carry

    ssum = sc_rows[0]
    for r in sc_rows[1:]:
        ssum = ssum + r
    wscale = ROUTED_SCALE / ssum
    nb = tt // LANES
    for k in range(TOP_K):
        rk = jnp.sum(jnp.where(ei == idx_rows[k], pos, 0.0), axis=0, keepdims=True).astype(i32)
        for j in range(nb):
            eidx_ref[j, k:k + 1, :] = idx_rows[k][:, j * LANES:(j + 1) * LANES].astype(i32)
            rank_ref[j, k:k + 1, :] = rk[:, j * LANES:(j + 1) * LANES]
    wstack = jnp.concatenate([r * wscale for r in sc_rows] + [jnp.zeros((LANES - TOP_K, tt), f32)], axis=0)
    for j in range(nb):
        blk = wstack[:, j * LANES:(j + 1) * LANES].T
        wcol_ref[j * LANES:(j + 1) * LANES, :] = blk[:, 0:TOP_K]


def _route(x1, mod_f, g_pre, w_router, router_bias, *, tt):
    B, S, D = x1.shape
    T = B * S
    E = w_router.shape[1]
    assert S % tt == 0 and tt % LANES == 0 and E % (N_EXPERT_GROUPS * SUBLANES) == 0
    spt = S // tt
    nb = tt // LANES
    kern = functools.partial(_route_kernel, tt=tt, D=D, E=E)
    return pl.pallas_call(
        kern,
        grid=(T // tt,),
        in_specs=[pl.BlockSpec((tt, D), lambda i: (i, 0)),
                  pl.BlockSpec((1, 2, D), lambda i: (i // spt, 0, 0)),
                  pl.BlockSpec((1, D), lambda i: (0, 0)),
                  pl.BlockSpec((E, D), lambda i: (0, 0)),
                  pl.BlockSpec((E, 1), lambda i: (0, 0))],
        out_specs=[pl.BlockSpec((tt, D // 2), lambda i: (i, 0)),
                   pl.BlockSpec((nb, TOP_K, LANES), lambda i: (i, 0, 0)),
                   pl.BlockSpec((nb, TOP_K, LANES), lambda i: (i, 0, 0)),
                   pl.BlockSpec((tt, TOP_K), lambda i: (i, 0)),
                   pl.BlockSpec((E, LANES), lambda i: (0, 0))],
        out_shape=[jax.ShapeDtypeStruct((T, D // 2), i32),
                   jax.ShapeDtypeStruct((T // LANES, TOP_K, LANES), i32),
                   jax.ShapeDtypeStruct((T // LANES, TOP_K, LANES), i32),
                   jax.ShapeDtypeStruct((T, TOP_K), f32),
                   jax.ShapeDtypeStruct((E, LANES), f32)],
        scratch_shapes=[pltpu.VMEM((E, LANES), f32)],
        compiler_params=pltpu.CompilerParams(dimension_semantics=("arbitrary",), vmem_limit_bytes=VMEM_LIMIT),
        name="route",
    )(x1.reshape(T, D), mod_f, g_pre.reshape(1, D), w_router.T.astype(bf16), router_bias.reshape(E, 1))


def _plan_kernel(cnt_ref, eidx_ref, rank_ref, dest_ref, tile_e_ref, ntiles_ref, pstart_ref, *, E, TM, NT, nb):
    i = pl.program_id(0)

    @pl.when(i == 0)
    def _():
        cnt_col = cnt_ref[:, 0:1]
        pad_col = jnp.ceil(cnt_col / TM) * TM
        r = lax.broadcasted_iota(i32, (E, E), 0)
        c = lax.broadcasted_iota(i32, (E, E), 1)
        pad_row = jnp.sum(jnp.where(r == c, pad_col, 0.0), axis=0, keepdims=True)
        pstart = jnp.sum(jnp.where(c < r, pad_row, 0.0), axis=1, keepdims=True)
        pstart_ref[...] = jnp.broadcast_to(pstart, (E, LANES))
        pend = pstart + pad_col
        tile_row = (lax.broadcasted_iota(i32, (1, NT), 1) * TM).astype(f32)
        te = jnp.sum((pend <= tile_row).astype(f32), axis=0, keepdims=True)
        tile_e_ref[...] = jnp.minimum(te, E - 1).astype(i32)
        total = jnp.sum(pad_col, axis=0, keepdims=True)
        ntiles_ref[...] = jnp.broadcast_to(total / TM, (1, LANES)).astype(i32)

    ei = lax.broadcasted_iota(i32, (E, LANES), 0)
    pstart = pstart_ref[...]
    for j in range(nb):
        for k in range(TOP_K):
            e_row = eidx_ref[j, k:k + 1, :]
            base = jnp.sum(jnp.where(ei == e_row, pstart, 0.0), axis=0, keepdims=True).astype(i32)
            dest_ref[j, k:k + 1, :] = base + rank_ref[j, k:k + 1, :]


def _plan(cnt, eidx, rank, *, TM, NT):
    E = cnt.shape[0]
    NB = eidx.shape[0]
    nb = 4
    assert NB % nb == 0
    kern = functools.partial(_plan_kernel, E=E, TM=TM, NT=NT, nb=nb)
    blk = pl.BlockSpec((nb, TOP_K, LANES), lambda i: (i, 0, 0))
    return pl.pallas_call(
        kern,
        grid=(NB // nb,),
        in_specs=[pl.BlockSpec((E, LANES), lambda i: (0, 0)), blk, blk],
        out_specs=[blk,
                   pl.BlockSpec((1, NT), lambda i: (0, 0)),
                   pl.BlockSpec((1, LANES), lambda i: (0, 0))],
        out_shape=[jax.ShapeDtypeStruct((NB, TOP_K, LANES), i32),
                   jax.ShapeDtypeStruct((1, NT), i32),
                   jax.ShapeDtypeStruct((1, LANES), i32)],
        scratch_shapes=[pltpu.VMEM((E, LANES), f32)],
        compiler_params=pltpu.CompilerParams(dimension_semantics=("arbitrary",)),
        name="plan",
    )(cnt, eidx, rank)


def _sc_workers():
    info = plsc.get_sparse_core_info()
    return info.num_cores, info.num_subcores


def _dispatch(h2p, dest, NP):
    T, W = h2p.shape
    NC, NS = _sc_workers()
    nblk = T // (NC * NS * SC_ROWS)
    assert T == nblk * NC * NS * SC_ROWS
    mesh = plsc.VectorSubcoreMesh(core_axis_name="c", subcore_axis_name="s")

    @functools.partial(
        pl.kernel, mesh=mesh, out_type=jax.ShapeDtypeStruct((NP, W), h2p.dtype),
        scratch_types=[pltpu.VMEM((TOP_K, SC_ROWS), i32), pltpu.VMEM((SC_ROWS, W), h2p.dtype)],
        name="dispatch")
    def kern(h_hbm, dest_hbm, xs_hbm, idx_v, rows_v):
        wid = lax.axis_index("s") * NC + lax.axis_index("c")

        @pl.loop(0, nblk)
        def _(b):
            blk = wid * nblk + b
            pltpu.sync_copy(dest_hbm.at[blk], idx_v)
            pltpu.sync_copy(h_hbm.at[pl.ds(pl.multiple_of(blk * SC_ROWS, SC_ROWS), SC_ROWS)], rows_v)
            for k in range(TOP_K):
                pltpu.sync_copy(rows_v, xs_hbm.at[idx_v.at[k]])

    return kern(h2p, dest)


def _collect(ys, dest, T):
    NP, W = ys.shape
    NC, NS = _sc_workers()
    nblk = T // (NC * NS * SC_ROWS)
    assert T == nblk * NC * NS * SC_ROWS
    mesh = plsc.VectorSubcoreMesh(core_axis_name="c", subcore_axis_name="s")

    @functools.partial(
        pl.kernel, mesh=mesh, out_type=jax.ShapeDtypeStruct((TOP_K, T, W), ys.dtype),
        scratch_types=[pltpu.VMEM((TOP_K, SC_ROWS), i32), pltpu.VMEM((SC_ROWS, W), ys.dtype)],
        name="collect")
    def kern(ys_hbm, dest_hbm, yg_hbm, idx_v, rows_v):
        wid = lax.axis_index("s") * NC + lax.axis_index("c")

        @pl.loop(0, nblk)
        def _(b):
            blk = wid * nblk + b
            pltpu.sync_copy(dest_hbm.at[blk], idx_v)
            for k in range(TOP_K):
                pltpu.sync_copy(ys_hbm.at[idx_v.at[k]], rows_v)
                pltpu.sync_copy(rows_v, yg_hbm.at[k, pl.ds(pl.multiple_of(blk * SC_ROWS, SC_ROWS), SC_ROWS)])

    return kern(ys, dest)


def _experts_kernel(tile_e_ref, ntiles_ref, xs_ref, wgu_ref, wd_ref, ys_ref, wgu_bf, wd_bf, *, ED, D):
    i = pl.program_id(0)
    nt = ntiles_ref[0]

    @pl.when(i < nt)
    def _():
        prev = tile_e_ref[jnp.maximum(i - 1, 0)]

        @pl.when((i == 0) | (tile_e_ref[i] != prev))
        def _():
            wgu_bf[...] = wgu_ref[0].astype(bf16)
            wd_bf[...] = wd_ref[0].astype(bf16)

        lo, hi = _unpack_bf16_pair(xs_ref[...])
        gu = (jnp.dot(lo.astype(bf16), wgu_bf[0:D // 2, :], preferred_element_type=f32)
              + jnp.dot(hi.astype(bf16), wgu_bf[D // 2:, :], preferred_element_type=f32))
        act = (_silu(gu[:, :ED]) * gu[:, ED:]).astype(bf16)
        y = jnp.dot(act, wd_bf[...], preferred_element_type=f32)
        ys_ref[...] = _pack_bf16_pair(y[:, :D // 2], y[:, D // 2:])


def _experts(xs, tile_e, ntiles, w_gate_up, w_down, *, TM):
    NP, W = xs.shape
    E, D, ED2 = w_gate_up.shape
    ED = ED2 // 2
    NT = NP // TM
    kern = functools.partial(_experts_kernel, ED=ED, D=D)

    def row_map(i, te, nt):
        return (jnp.minimum(i, nt[0] - 1), 0)

    def w_map(i, te, nt):
        return (te[jnp.minimum(i, nt[0] - 1)], 0, 0)

    return pl.pallas_call(
        kern,
        grid_spec=pltpu.PrefetchScalarGridSpec(
            num_scalar_prefetch=2,
            grid=(NT,),
            in_specs=[pl.BlockSpec((TM, W), row_map),
                      pl.BlockSpec((1, D, ED2), w_map),
                      pl.BlockSpec((1, ED, D), w_map)],
            out_specs=pl.BlockSpec((TM, W), row_map),
            scratch_shapes=[pltpu.VMEM((D, ED2), bf16), pltpu.VMEM((ED, D), bf16)]),
        out_shape=jax.ShapeDtypeStruct((NP, W), i32),
        compiler_params=pltpu.CompilerParams(dimension_semantics=("arbitrary",), vmem_limit_bytes=VMEM_LIMIT),
        name="experts",
    )(tile_e, ntiles, xs, w_gate_up, w_down)


def _final_kernel(yg_ref, wcol_ref, h2p_ref, x1_ref, mod_ref, gpost_ref, wsgu_ref, wsd_ref, o_ref, *, D, SD):
    wcol = wcol_ref[...]
    r_lo = jnp.zeros(h2p_ref.shape, f32)
    r_hi = jnp.zeros(h2p_ref.shape, f32)
    for k in range(TOP_K):
        lo, hi = _unpack_bf16_pair(yg_ref[k])
        wk = wcol[:, k:k + 1]
        r_lo = r_lo + wk * lo
        r_hi = r_hi + wk * hi
    routed = jnp.concatenate([r_lo, r_hi], axis=-1)

    hlo, hhi = _unpack_bf16_pair(h2p_ref[...])
    gs = (jnp.dot(hlo.astype(bf16), wsgu_ref[0:D // 2, :], preferred_element_type=f32)
          + jnp.dot(hhi.astype(bf16), wsgu_ref[D // 2:, :], preferred_element_type=f32))
    sh = (_silu(gs[:, :SD]) * gs[:, SD:]).astype(bf16)
    y = routed + jnp.dot(sh, wsd_ref[...], preferred_element_type=f32)
    yn = y * lax.rsqrt(jnp.mean(y * y, axis=-1, keepdims=True) + EPS) * gpost_ref[...]
    o_ref[...] = x1_ref[...] + mod_ref[0, 2:3, :] * yn


def _final(yg, wcol, h2p, x1, mod_f, g_post, ws_gate_up, ws_down, *, tt):
    B, S, D = x1.shape
    T = B * S
    SD = ws_down.shape[0]
    spt = S // tt
    kern = functools.partial(_final_kernel, D=D, SD=SD)
    out = pl.pallas_call(
        kern,
        grid=(T // tt,),
        in_specs=[pl.BlockSpec((TOP_K, tt, D // 2), lambda i: (0, i, 0)),
                  pl.BlockSpec((tt, TOP_K), lambda i: (i, 0)),
                  pl.BlockSpec((tt, D // 2), lambda i: (i, 0)),
                  pl.BlockSpec((tt, D), lambda i: (i, 0)),
                  pl.BlockSpec((1, 3, D), lambda i: (i // spt, 0, 0)),
                  pl.BlockSpec((1, D), lambda i: (0, 0)),
                  pl.BlockSpec((D, 2 * SD), lambda i: (0, 0)),
                  pl.BlockSpec((SD, D), lambda i: (0, 0))],
        out_specs=pl.BlockSpec((tt, D), lambda i: (i, 0)),
        out_shape=jax.ShapeDtypeStruct((T, D), f32),
        compiler_params=pltpu.CompilerParams(dimension_semantics=("arbitrary",), vmem_limit_bytes=VMEM_LIMIT),
        name="final",
    )(yg, wcol, h2p, x1.reshape(T, D), mod_f, g_post.reshape(1, D), ws_gate_up.astype(bf16), ws_down.astype(bf16))
    return out.reshape(B, S, D)


def _layer(x, mod, g_pre_mix, g_post_mix, g_pre_ffn, g_post_ffn, w_in, w_dw, b_dw, ln_g, ln_b, w_conv_out,
           w_pool, b_pool, pool_scale, w_out, w_router, router_bias, w_gate_up, w_down, ws_gate_up, ws_down):
    B, S, D = x.shape
    T = B * S
    E = w_router.shape[1]
    mod6 = mod.reshape(B, 6, D)
    mod_m, mod_f = mod6[:, 0:3], mod6[:, 3:6]
    x1 = _mixer(x, mod_m, g_pre_mix, g_post_mix, w_in, w_dw, b_dw, ln_g, ln_b, w_conv_out, w_pool, b_pool,
                pool_scale, w_out, ts=min(256, S))
    h2p, eidx, rank, wcol, cnt = _route(x1, mod_f[:, 0:2], g_pre_ffn, w_router, router_bias, tt=min(512, S))
    TM = 512
    NT = (T * TOP_K) // TM + E
    dest, tile_e, ntiles = _plan(cnt, eidx, rank, TM=TM, NT=NT)
    xs = _dispatch(h2p, dest, NT * TM)
    ys = _experts(xs, tile_e.reshape(NT), ntiles.reshape(LANES)[0:1], w_gate_up, w_down, TM=TM)
    yg = _collect(ys, dest, T)
    return _final(yg, wcol, h2p, x1, mod_f, g_post_ffn, ws_gate_up, ws_down, tt=min(256, S))


def kernel(x, c, w_ada, b_ada, g_pre_mix, g_post_mix, g_pre_ffn, g_post_ffn, w_in, w_dw, b_dw, ln_g, ln_b,
           w_conv_out, w_pool, b_pool, pool_scale, w_out, w_router, router_bias, w_gate_up, w_down, ws_gate_up,
           ws_down):
    depth = w_ada.shape[0]
    for l in range(depth):
        mod = _adaln(c, w_ada[l], b_ada[l])
        x = _layer(x, mod, g_pre_mix[l], g_post_mix[l], g_pre_ffn[l], g_post_ffn[l], w_in[l], w_dw[l], b_dw[l],
                   ln_g[l], ln_b[l], w_conv_out[l], w_pool[l], b_pool[l], pool_scale[l], w_out[l], w_router[l],
                   router_bias[l], w_gate_up[l], w_down[l], ws_gate_up[l], ws_down[l])
    return x
```

```python
import functools

import jax
import jax.numpy as jnp
from jax import lax
from jax.experimental import pallas as pl
from jax.experimental.pallas import tpu as pltpu
from jax.experimental.pallas import tpu_sc as plsc

EPS = 1e-6
TOP_K = 8
N_EXPERT_GROUPS = 8
TOPK_GROUPS = 4
ROUTED_SCALE = 2.5
POOL_WINDOWS = (2, 4, 8, 16)

LANES = 128
SUBLANES = 8
CONV_HIST = 32
POOL_HIST = 16
CONV_ROWS = 32
SC_ROWS = 128
VMEM_LIMIT = 56 * 1024 * 1024

f32 = jnp.float32
bf16 = jnp.bfloat16
i32 = jnp.int32


def _sigmoid(v):
    return 0.5 * jnp.tanh(0.5 * v) + 0.5


def _silu(v):
    return v * _sigmoid(v)


def _pack_bf16_pair(lo, hi):
    lo_bits = lax.bitcast_convert_type(lo.astype(bf16).astype(f32), jnp.uint32)
    hi_bits = lax.bitcast_convert_type(hi.astype(bf16).astype(f32), jnp.uint32)
    word = lax.shift_right_logical(lo_bits, jnp.uint32(16)) | hi_bits
    return lax.bitcast_convert_type(word, i32)


def _unpack_bf16_pair(word):
    w = lax.bitcast_convert_type(word, jnp.uint32)
    lo = lax.bitcast_convert_type(lax.shift_left(w, jnp.uint32(16)), f32)
    hi = lax.bitcast_convert_type(w & jnp.uint32(0xFFFF0000), f32)
    return lo, hi


def _adaln_kernel(c_ref, w_ref, b_ref, o_ref):
    cond = _silu(c_ref[...])
    o_ref[...] = jnp.dot(cond.astype(bf16), w_ref[...].astype(bf16), preferred_element_type=f32) + b_ref[...]


def _adaln(c, w_ada, b_ada):
    B, D = c.shape
    N = w_ada.shape[1]
    tn = 1024
    return pl.pallas_call(
        _adaln_kernel,
        grid=(N // tn,),
        in_specs=[pl.BlockSpec((B, D), lambda j: (0, 0)),
                  pl.BlockSpec((D, tn), lambda j: (0, j)),
                  pl.BlockSpec((1, tn), lambda j: (0, j))],
        out_specs=pl.BlockSpec((B, tn), lambda j: (0, j)),
        out_shape=jax.ShapeDtypeStruct((B, N), f32),
        name="adaln",
    )(c, w_ada, b_ada.reshape(1, N))


def _mixer_kernel(x_ref, mod_ref, gpre_ref, gpost_ref, win_ref, wdw_ref, bdw_ref, lng_ref, lnb_ref,
                  wco_ref, wpool_ref, bpool_ref, pscale_ref, wout_ref, o_ref,
                  aext_ref, ashift_ref, pext_ref, act_ref, *, ts, D, conv_k):
    s = pl.program_id(1)

    @pl.when(s == 0)
    def _():
        aext_ref[0:CONV_HIST, :] = jnp.zeros((CONV_HIST, D), f32)
        pext_ref[0:POOL_HIST, :] = jnp.zeros((POOL_HIST, D), f32)

    x = x_ref[0]
    shift, scale, gate = mod_ref[0, 0:1, :], mod_ref[0, 1:2, :], mod_ref[0, 2:3, :]
    h = x * lax.rsqrt(jnp.mean(x * x, axis=-1, keepdims=True) + EPS) * gpre_ref[...]
    hb = (h * (1.0 + scale) + shift).astype(bf16)

    a_val = jnp.dot(hb, win_ref[:, 0:D], preferred_element_type=f32)
    a_gate = jnp.dot(hb, win_ref[:, D:2 * D], preferred_element_type=f32)
    aext_ref[CONV_HIST:CONV_HIST + ts, :] = a_val * _sigmoid(a_gate)
    n_ext = ts + CONV_HIST
    a_full = aext_ref[...]
    for r in range(1, SUBLANES):
        ashift_ref[r - 1] = pltpu.roll(a_full, n_ext - r, axis=0)
    rc = CONV_ROWS
    off0 = CONV_HIST - (conv_k - 1)
    for c in range(ts // rc):
        acc = jnp.broadcast_to(bdw_ref[...], (rc, D))
        for k in range(conv_k):
            r, q = (off0 + k) % SUBLANES, (off0 + k) // SUBLANES
            row0 = c * rc + q * SUBLANES
            tap = aext_ref[row0:row0 + rc, :] if r == 0 else ashift_ref[r - 1, row0:row0 + rc, :]
            acc = acc + jnp.concatenate([wdw_ref[k]] * (rc // SUBLANES), axis=0) * tap
        mu = jnp.mean(acc, axis=-1, keepdims=True)
        cen = acc - mu
        var = jnp.mean(cen * cen, axis=-1, keepdims=True)
        ln = cen * lax.rsqrt(var + EPS) * lng_ref[...] + lnb_ref[...]
        act_ref[c * rc:(c + 1) * rc, :] = _silu(ln).astype(bf16)
    aext_ref[0:CONV_HIST, :] = aext_ref[ts:ts + CONV_HIST, :]
    a_out = jnp.dot(act_ref[...], wco_ref[...], preferred_element_type=f32)

    pext_ref[POOL_HIST:POOL_HIST + ts, :] = jnp.dot(hb, win_ref[:, 2 * D:3 * D], preferred_element_type=f32)
    t_glob = s * ts + lax.broadcasted_iota(i32, (ts, 1), 0)
    dg = D // len(POOL_WINDOWS)
    bms = []
    for g, win in enumerate(POOL_WINDOWS):
        cols = slice(g * dg, (g + 1) * dg)
        p_ext = pext_ref[:, cols]
        wsum, j = p_ext, 1
        while j < win:
            wsum = wsum + pltpu.roll(wsum, j, axis=0)
            j *= 2
        cur = p_ext[POOL_HIST:, :]
        cnt = jnp.minimum(t_glob + 1, win).astype(f32)
        pooled = wsum[POOL_HIST:, :] / cnt - cur
        bms.append(jnp.dot(pooled.astype(bf16), wpool_ref[g], preferred_element_type=f32))
    pext_ref[0:POOL_HIST, :] = pext_ref[ts:ts + POOL_HIST, :]
    bm = (jnp.concatenate(bms, axis=-1) + bpool_ref[...]) * pscale_ref[...]

    g_a = _sigmoid(jnp.dot(hb, win_ref[:, 3 * D:4 * D], preferred_element_type=f32))
    g_b = _sigmoid(jnp.dot(hb, win_ref[:, 4 * D:5 * D], preferred_element_type=f32))
    mixed = (g_a * a_out + g_b * bm).astype(bf16)
    y = jnp.dot(mixed, wout_ref[...], preferred_element_type=f32)
    yn = y * lax.rsqrt(jnp.mean(y * y, axis=-1, keepdims=True) + EPS) * gpost_ref[...]
    o_ref[0] = x + gate * yn


def _mixer(x, mod_m, g_pre, g_post, w_in, w_dw, b_dw, ln_g, ln_b, w_conv_out, w_pool, b_pool, pool_scale,
           w_out, *, ts):
    B, S, D = x.shape
    conv_k = w_dw.shape[0]
    assert conv_k - 1 <= CONV_HIST and max(POOL_WINDOWS) - 1 <= POOL_HIST
    assert S % ts == 0 and ts >= CONV_HIST and D % (LANES * len(POOL_WINDOWS)) == 0
    row = lambda v: v.reshape(1, D)
    vm = pl.BlockSpec(memory_space=pltpu.VMEM)
    kern = functools.partial(_mixer_kernel, ts=ts, D=D, conv_k=conv_k)
    return pl.pallas_call(
        kern,
        grid=(B, S // ts),
        in_specs=[pl.BlockSpec((1, ts, D), lambda b, s: (b, s, 0)),
                  pl.BlockSpec((1, 3, D), lambda b, s: (b, 0, 0)),
                  vm, vm, vm, vm, vm, vm, vm, vm, vm, vm, vm, vm],
        out_specs=pl.BlockSpec((1, ts, D), lambda b, s: (b, s, 0)),
        out_shape=jax.ShapeDtypeStruct((B, S, D), f32),
        scratch_shapes=[pltpu.VMEM((ts + CONV_HIST, D), f32),
                        pltpu.VMEM((SUBLANES - 1, ts + CONV_HIST, D), f32),
                        pltpu.VMEM((ts + POOL_HIST, D), f32),
                        pltpu.VMEM((ts, D), bf16)],
        compiler_params=pltpu.CompilerParams(dimension_semantics=("arbitrary", "arbitrary"),
                                             vmem_limit_bytes=VMEM_LIMIT),
        name="mixer",
    )(x, mod_m, row(g_pre), row(g_post), w_in.astype(bf16),
      jnp.broadcast_to(w_dw[:, None, :], (conv_k, SUBLANES, D)), row(b_dw), row(ln_g), row(ln_b),
      w_conv_out.astype(bf16), w_pool.astype(bf16), row(b_pool), row(pool_scale), w_out.astype(bf16))


def _route_kernel(x_ref, mod_ref, gpre_ref, wr_ref, bias_ref,
                  h2p_ref, eidx_ref, rank_ref, wcol_ref, cnt_ref, carry_ref, *, tt, D, E):
    i = pl.program_id(0)

    @pl.when(i == 0)
    def _():
        carry_ref[...] = jnp.zeros((E, LANES), f32)

    x = x_ref[...]
    shift, scale = mod_ref[0, 0:1, :], mod_ref[0, 1:2, :]
    h = x * lax.rsqrt(jnp.mean(x * x, axis=-1, keepdims=True) + EPS) * gpre_ref[...]
    h = h * (1.0 + scale) + shift
    h2p_ref[...] = _pack_bf16_pair(h[:, :D // 2], h[:, D // 2:])

    logits = lax.dot_general(wr_ref[...], h.astype(bf16), (((1,), (1,)), ((), ())),
                             preferred_element_type=f32)
    scores = _sigmoid(logits)
    choice = scores + bias_ref[...]
    G = N_EXPERT_GROUPS
    ge = E // G
    neg = jnp.float32(-jnp.inf)

    ch3 = choice.reshape(G, ge, tt)
    m1 = jnp.max(ch3, axis=1, keepdims=True)
    is_m1 = ch3 == m1
    n_m1 = jnp.sum(is_m1.astype(f32), axis=1, keepdims=True)
    m2 = jnp.max(jnp.where(is_m1, neg, ch3), axis=1, keepdims=True)
    gs = (m1 + jnp.where(n_m1 >= 2.0, m1, m2)).reshape(G, tt)

    gi = lax.broadcasted_iota(i32, (G, tt), 0).astype(f32)
    gsel = jnp.zeros((G, tt), f32)
    for _ in range(TOPK_GROUPS):
        gm = jnp.max(gs, axis=0, keepdims=True)
        first = jnp.min(jnp.where(gs == gm, gi, float(G)), axis=0, keepdims=True)
        hit = gi == first
        gsel = jnp.where(hit, 1.0, gsel)
        gs = jnp.where(hit, neg, gs)

    emask = jnp.broadcast_to(gsel.reshape(G, 1, tt), (G, ge, tt)).reshape(E, tt)
    masked = jnp.where(emask > 0.5, choice, neg)
    ei = lax.broadcasted_iota(i32, (E, tt), 0).astype(f32)
    sel = jnp.zeros((E, tt), f32)
    idx_rows, sc_rows = [], []
    for _ in range(TOP_K):
        mx = jnp.max(masked, axis=0, keepdims=True)
        first = jnp.min(jnp.where(masked == mx, ei, float(E)), axis=0, keepdims=True)
        hit = ei == first
        idx_rows.append(first)
        sc_rows.append(jnp.sum(jnp.where(hit, scores, 0.0), axis=0, keepdims=True))
        sel = sel + hit.astype(f32)
        masked = jnp.where(hit, neg, masked)

    upper = (lax.broadcasted_iota(i32, (tt, tt), 0) < lax.broadcasted_iota(i32, (tt, tt), 1)).astype(bf16)
    selb = sel.astype(bf16)
    excl = jnp.dot(selb, upper, preferred_element_type=f32)
    carry = carry_ref[...]
    pos = excl + jnp.concatenate([carry] * (tt // LANES), axis=1)
    carry = carry + jnp.dot(selb, jnp.ones((tt, LANES), bf16), preferred_element_type=f32)
    carry_ref[...] = carry
    cnt_ref[...] = carry

    ssum = sc_rows[0]
    for r in sc_rows[1:]:
        ssum = ssum + r
    wscale = ROUTED_SCALE / ssum
    nb = tt // LANES
    for k in range(TOP_K):
        rk = jnp.sum(jnp.where(ei == idx_rows[k], pos, 0.0), axis=0, keepdims=True).astype(i32)
        for j in range(nb):
            eidx_ref[j, k:k + 1, :] = idx_rows[k][:, j * LANES:(j + 1) * LANES].astype(i32)
            rank_ref[j, k:k + 1, :] = rk[:, j * LANES:(j + 1) * LANES]
    wstack = jnp.concatenate([r * wscale for r in sc_rows] + [jnp.zeros((LANES - TOP_K, tt), f32)], axis=0)
    for j in range(nb):
        blk = wstack[:, j * LANES:(j + 1) * LANES].T
        wcol_ref[j * LANES:(j + 1) * LANES, :] = blk[:, 0:TOP_K]


def _route(x1, mod_f, g_pre, w_router, router_bias, *, tt):
    B, S, D = x1.shape
    T = B * S
    E = w_router.shape[1]
    assert S % tt == 0 and tt % LANES == 0 and E % (N_EXPERT_GROUPS * SUBLANES) == 0
    spt = S // tt
    nb = tt // LANES
    kern = functools.partial(_route_kernel, tt=tt, D=D, E=E)
    return pl.pallas_call(
        kern,
        grid=(T // tt,),
        in_specs=[pl.BlockSpec((tt, D), lambda i: (i, 0)),
                  pl.BlockSpec((1, 2, D), lambda i: (i // spt, 0, 0)),
                  pl.BlockSpec((1, D), lambda i: (0, 0)),
                  pl.BlockSpec((E, D), lambda i: (0, 0)),
                  pl.BlockSpec((E, 1), lambda i: (0, 0))],
        out_specs=[pl.BlockSpec((tt, D // 2), lambda i: (i, 0)),
                   pl.BlockSpec((nb, TOP_K, LANES), lambda i: (i, 0, 0)),
                   pl.BlockSpec((nb, TOP_K, LANES), lambda i: (i, 0, 0)),
                   pl.BlockSpec((tt, TOP_K), lambda i: (i, 0)),
                   pl.BlockSpec((E, LANES), lambda i: (0, 0))],
        out_shape=[jax.ShapeDtypeStruct((T, D // 2), i32),
                   jax.ShapeDtypeStruct((T // LANES, TOP_K, LANES), i32),
                   jax.ShapeDtypeStruct((T // LANES, TOP_K, LANES), i32),
                   jax.ShapeDtypeStruct((T, TOP_K), f32),
                   jax.ShapeDtypeStruct((E, LANES), f32)],
        scratch_shapes=[pltpu.VMEM((E, LANES), f32)],
        compiler_params=pltpu.CompilerParams(dimension_semantics=("arbitrary",), vmem_limit_bytes=VMEM_LIMIT),
        name="route",
    )(x1.reshape(T, D), mod_f, g_pre.reshape(1, D), w_router.T.astype(bf16), router_bias.reshape(E, 1))


def _plan_kernel(cnt_ref, eidx_ref, rank_ref, dest_ref, tile_e_ref, ntiles_ref, pstart_ref, *, E, TM, NT, nb):
    i = pl.program_id(0)

    @pl.when(i == 0)
    def _():
        cnt_col = cnt_ref[:, 0:1]
        pad_col = jnp.ceil(cnt_col / TM) * TM
        r = lax.broadcasted_iota(i32, (E, E), 0)
        c = lax.broadcasted_iota(i32, (E, E), 1)
        pad_row = jnp.sum(jnp.where(r == c, pad_col, 0.0), axis=0, keepdims=True)
        pstart = jnp.sum(jnp.where(c < r, pad_row, 0.0), axis=1, keepdims=True)
        pstart_ref[...] = jnp.broadcast_to(pstart, (E, LANES))
        pend = pstart + pad_col
        tile_row = (lax.broadcasted_iota(i32, (1, NT), 1) * TM).astype(f32)
        te = jnp.sum((pend <= tile_row).astype(f32), axis=0, keepdims=True)
        tile_e_ref[...] = jnp.minimum(te, E - 1).astype(i32)
        total = jnp.sum(pad_col, axis=0, keepdims=True)
        ntiles_ref[...] = jnp.broadcast_to(total / TM, (1, LANES)).astype(i32)

    ei = lax.broadcasted_iota(i32, (E, LANES), 0)
    pstart = pstart_ref[...]
    for j in range(nb):
        for k in range(TOP_K):
            e_row = eidx_ref[j, k:k + 1, :]
            base = jnp.sum(jnp.where(ei == e_row, pstart, 0.0), axis=0, keepdims=True).astype(i32)
            dest_ref[j, k:k + 1, :] = base + rank_ref[j, k:k + 1, :]


def _plan(cnt, eidx, rank, *, TM, NT):
    E = cnt.shape[0]
    NB = eidx.shape[0]
    nb = 4
    assert NB % nb == 0
    kern = functools.partial(_plan_kernel, E=E, TM=TM, NT=NT, nb=nb)
    blk = pl.BlockSpec((nb, TOP_K, LANES), lambda i: (i, 0, 0))
    return pl.pallas_call(
        kern,
        grid=(NB // nb,),
        in_specs=[pl.BlockSpec((E, LANES), lambda i: (0, 0)), blk, blk],
        out_specs=[blk,
                   pl.BlockSpec((1, NT), lambda i: (0, 0)),
                   pl.BlockSpec((1, LANES), lambda i: (0, 0))],
        out_shape=[jax.ShapeDtypeStruct((NB, TOP_K, LANES), i32),
                   jax.ShapeDtypeStruct((1, NT), i32),
                   jax.ShapeDtypeStruct((1, LANES), i32)],
        scratch_shapes=[pltpu.VMEM((E, LANES), f32)],
        compiler_params=pltpu.CompilerParams(dimension_semantics=("arbitrary",)),
        name="plan",
    )(cnt, eidx, rank)


def _sc_workers():
    info = plsc.get_sparse_core_info()
    return info.num_cores, info.num_subcores


def _dispatch(h2p, dest, NP):
    T, W = h2p.shape
    NC, NS = _sc_workers()
    nblk = T // (NC * NS * SC_ROWS)
    assert T == nblk * NC * NS * SC_ROWS
    mesh = plsc.VectorSubcoreMesh(core_axis_name="c", subcore_axis_name="s")

    @functools.partial(
        pl.kernel, mesh=mesh, out_type=jax.ShapeDtypeStruct((NP, W), h2p.dtype),
        scratch_types=[pltpu.VMEM((TOP_K, SC_ROWS), i32), pltpu.VMEM((SC_ROWS, W), h2p.dtype)],
        name="dispatch")
    def kern(h_hbm, dest_hbm, xs_hbm, idx_v, rows_v):
        wid = lax.axis_index("s") * NC + lax.axis_index("c")

        @pl.loop(0, nblk)
        def _(b):
            blk = wid * nblk + b
            pltpu.sync_copy(dest_hbm.at[blk], idx_v)
            pltpu.sync_copy(h_hbm.at[pl.ds(pl.multiple_of(blk * SC_ROWS, SC_ROWS), SC_ROWS)], rows_v)
            for k in range(TOP_K):
                pltpu.sync_copy(rows_v, xs_hbm.at[idx_v.at[k]])

    return kern(h2p, dest)


def _collect(ys, dest, T):
    NP, W = ys.shape
    NC, NS = _sc_workers()
    nblk = T // (NC * NS * SC_ROWS)
    assert T == nblk * NC * NS * SC_ROWS
    mesh = plsc.VectorSubcoreMesh(core_axis_name="c", subcore_axis_name="s")

    @functools.partial(
        pl.kernel, mesh=mesh, out_type=jax.ShapeDtypeStruct((TOP_K, T, W), ys.dtype),
        scratch_types=[pltpu.VMEM((TOP_K, SC_ROWS), i32), pltpu.VMEM((SC_ROWS, W), ys.dtype)],
        name="collect")
    def kern(ys_hbm, dest_hbm, yg_hbm, idx_v, rows_v):
        wid = lax.axis_index("s") * NC + lax.axis_index("c")

        @pl.loop(0, nblk)
        def _(b):
            blk = wid * nblk + b
            pltpu.sync_copy(dest_hbm.at[blk], idx_v)
            for k in range(TOP_K):
                pltpu.sync_copy(ys_hbm.at[idx_v.at[k]], rows_v)
                pltpu.sync_copy(rows_v, yg_hbm.at[k, pl.ds(pl.multiple_of(blk * SC_ROWS, SC_ROWS), SC_ROWS)])

    return kern(ys, dest)


def _experts_kernel(tile_e_ref, ntiles_ref, xs_ref, wgu_ref, wd_ref, ys_ref, wgu_bf, wd_bf, *, ED, D):
    i = pl.program_id(0)
    nt = ntiles_ref[0]

    @pl.when(i < nt)
    def _():
        prev = tile_e_ref[jnp.maximum(i - 1, 0)]

        @pl.when((i == 0) | (tile_e_ref[i] != prev))
        def _():
            wgu_bf[...] = wgu_ref[0].astype(bf16)
            wd_bf[...] = wd_ref[0].astype(bf16)

        lo, hi = _unpack_bf16_pair(xs_ref[...])
        gu = (jnp.dot(lo.astype(bf16), wgu_bf[0:D // 2, :], preferred_element_type=f32)
              + jnp.dot(hi.astype(bf16), wgu_bf[D // 2:, :], preferred_element_type=f32))
        act = (_silu(gu[:, :ED]) * gu[:, ED:]).astype(bf16)
        y = jnp.dot(act, wd_bf[...], preferred_element_type=f32)
        ys_ref[...] = _pack_bf16_pair(y[:, :D // 2], y[:, D // 2:])


def _experts(xs, tile_e, ntiles, w_gate_up, w_down, *, TM):
    NP, W = xs.shape
    E, D, ED2 = w_gate_up.shape
    ED = ED2 // 2
    NT = NP // TM
    kern = functools.partial(_experts_kernel, ED=ED, D=D)

    def row_map(i, te, nt):
        return (jnp.minimum(i, nt[0] - 1), 0)

    def w_map(i, te, nt):
        return (te[jnp.minimum(i, nt[0] - 1)], 0, 0)

    return pl.pallas_call(
        kern,
        grid_spec=pltpu.PrefetchScalarGridSpec(
            num_scalar_prefetch=2,
            grid=(NT,),
            in_specs=[pl.BlockSpec((TM, W), row_map),
                      pl.BlockSpec((1, D, ED2), w_map),
                      pl.BlockSpec((1, ED, D), w_map)],
            out_specs=pl.BlockSpec((TM, W), row_map),
            scratch_shapes=[pltpu.VMEM((D, ED2), bf16), pltpu.VMEM((ED, D), bf16)]),
        out_shape=jax.ShapeDtypeStruct((NP, W), i32),
        compiler_params=pltpu.CompilerParams(dimension_semantics=("arbitrary",), vmem_limit_bytes=VMEM_LIMIT),
        name="experts",
    )(tile_e, ntiles, xs, w_gate_up, w_down)


def _final_kernel(yg_ref, wcol_ref, h2p_ref, x1_ref, mod_ref, gpost_ref, wsgu_ref, wsd_ref, o_ref, *, D, SD):
    wcol = wcol_ref[...]
    r_lo = jnp.zeros(h2p_ref.shape, f32)
    r_hi = jnp.zeros(h2p_ref.shape, f32)
    for k in range(TOP_K):
        lo, hi = _unpack_bf16_pair(yg_ref[k])
        wk = wcol[:, k:k + 1]
        r_lo = r_lo + wk * lo
        r_hi = r_hi + wk * hi
    routed = jnp.concatenate([r_lo, r_hi], axis=-1)

    hlo, hhi = _unpack_bf16_pair(h2p_ref[...])
    gs = (jnp.dot(hlo.astype(bf16), wsgu_ref[0:D // 2, :], preferred_element_type=f32)
          + jnp.dot(hhi.astype(bf16), wsgu_ref[D // 2:, :], preferred_element_type=f32))
    sh = (_silu(gs[:, :SD]) * gs[:, SD:]).astype(bf16)
    y = routed + jnp.dot(sh, wsd_ref[...], preferred_element_type=f32)
    yn = y * lax.rsqrt(jnp.mean(y * y, axis=-1, keepdims=True) + EPS) * gpost_ref[...]
    o_ref[...] = x1_ref[...] + mod_ref[0, 2:3, :] * yn


def _final(yg, wcol, h2p, x1, mod_f, g_post, ws_gate_up, ws_down, *, tt):
    B, S, D = x1.shape
    T = B * S
    SD = ws_down.shape[0]
    spt = S // tt
    kern = functools.partial(_final_kernel, D=D, SD=SD)
    out = pl.pallas_call(
        kern,
        grid=(T // tt,),
        in_specs=[pl.BlockSpec((TOP_K, tt, D // 2), lambda i: (0, i, 0)),
                  pl.BlockSpec((tt, TOP_K), lambda i: (i, 0)),
                  pl.BlockSpec((tt, D // 2), lambda i: (i, 0)),
                  pl.BlockSpec((tt, D), lambda i: (i, 0)),
                  pl.BlockSpec((1, 3, D), lambda i: (i // spt, 0, 0)),
                  pl.BlockSpec((1, D), lambda i: (0, 0)),
                  pl.BlockSpec((D, 2 * SD), lambda i: (0, 0)),
                  pl.BlockSpec((SD, D), lambda i: (0, 0))],
        out_specs=pl.BlockSpec((tt, D), lambda i: (i, 0)),
        out_shape=jax.ShapeDtypeStruct((T, D), f32),
        compiler_params=pltpu.CompilerParams(dimension_semantics=("arbitrary",), vmem_limit_bytes=VMEM_LIMIT),
        name="final",
    )(yg, wcol, h2p, x1.reshape(T, D), mod_f, g_post.reshape(1, D), ws_gate_up.astype(bf16), ws_down.astype(bf16))
    return out.reshape(B, S, D)


def _layer(x, mod, g_pre_mix, g_post_mix, g_pre_ffn, g_post_ffn, w_in, w_dw, b_dw, ln_g, ln_b, w_conv_out,
           w_pool, b_pool, pool_scale, w_out, w_router, router_bias, w_gate_up, w_down, ws_gate_up, ws_down):
    B, S, D = x.shape
    T = B * S
    E = w_router.shape[1]
    mod6 = mod.reshape(B, 6, D)
    mod_m, mod_f = mod6[:, 0:3], mod6[:, 3:6]
    x1 = _mixer(x, mod_m, g_pre_mix, g_post_mix, w_in, w_dw, b_dw, ln_g, ln_b, w_conv_out, w_pool, b_pool,
                pool_scale, w_out, ts=min(256, S))
    h2p, eidx, rank, wcol, cnt = _route(x1, mod_f[:, 0:2], g_pre_ffn, w_router, router_bias, tt=min(512, S))
    TM = 1024
    NT = (T * TOP_K) // TM + E
    dest, tile_e, ntiles = _plan(cnt, eidx, rank, TM=TM, NT=NT)
    xs = _dispatch(h2p, dest, NT * TM)
    ys = _experts(xs, tile_e.reshape(NT), ntiles.reshape(LANES)[0:1], w_gate_up, w_down, TM=TM)
    yg = _collect(ys, dest, T)
    return _final(yg, wcol, h2p, x1, mod_f, g_post_ffn, ws_gate_up, ws_down, tt=min(256, S))


def kernel(x, c, w_ada, b_ada, g_pre_mix, g_post_mix, g_pre_ffn, g_post_ffn, w_in, w_dw, b_dw, ln_g, ln_b,
           w_conv_out, w_pool, b_pool, pool_scale, w_out, w_router, router_bias, w_gate_up, w_down, ws_gate_up,
           ws_down):
    depth = w_ada.shape[0]
    for l in range(depth):
        mod = _adaln(c, w_ada[l], b_ada[l])
        x = _layer(x, mod, g_pre_mix[l], g_post_mix[l], g_pre_ffn[l], g_post_ffn[l], w_in[l], w_dw[l], b_dw[l],
                   ln_g[l], ln_b[l], w_conv_out[l], w_pool[l], b_pool[l], pool_scale[l], w_out[l], w_router[l],
                   router_bias[l], w_gate_up[l], w_down[l], ws_gate_up[l], ws_down[l])
    return x
```

```python
import functools

import jax
import jax.numpy as jnp
from jax import lax
from jax.experimental import pallas as pl
from jax.experimental.pallas import tpu as pltpu
from jax.experimental.pallas import tpu_sc as plsc

EPS = 1e-6
TOP_K = 8
N_EXPERT_GROUPS = 8
TOPK_GROUPS = 4
ROUTED_SCALE = 2.5
POOL_WINDOWS = (2, 4, 8, 16)

LANES = 128
SUBLANES = 8
CONV_HIST = 32
POOL_HIST = 16
CONV_ROWS = 32
N_IN = 3
N_OUT = 2
N_SUB = 2
N_TAIL_CHUNKS = 4
SC_ROWS = 128
VMEM_LIMIT = 56 * 1024 * 1024

f32 = jnp.float32
bf16 = jnp.bfloat16
i32 = jnp.int32


def _sigmoid(v):
    return 0.5 * jnp.tanh(0.5 * v) + 0.5


def _silu(v):
    return v * _sigmoid(v)


def _pack_bf16_pair(lo, hi):
    lo_bits = lax.bitcast_convert_type(lo.astype(bf16).astype(f32), jnp.uint32)
    hi_bits = lax.bitcast_convert_type(hi.astype(bf16).astype(f32), jnp.uint32)
    word = lax.shift_right_logical(lo_bits, jnp.uint32(16)) | hi_bits
    return lax.bitcast_convert_type(word, i32)


def _unpack_bf16_pair(word):
    w = lax.bitcast_convert_type(word, jnp.uint32)
    lo = lax.bitcast_convert_type(lax.shift_left(w, jnp.uint32(16)), f32)
    hi = lax.bitcast_convert_type(w & jnp.uint32(0xFFFF0000), f32)
    return lo, hi


def _adaln_kernel(c_ref, w_ref, b_ref, o_ref):
    cond = _silu(c_ref[...])
    o_ref[...] = jnp.dot(cond.astype(bf16), w_ref[...].astype(bf16), preferred_element_type=f32) + b_ref[...]


def _adaln(c, w_ada, b_ada):
    B, D = c.shape
    N = w_ada.shape[1]
    tn = 1024
    return pl.pallas_call(
        _adaln_kernel,
        grid=(N // tn,),
        in_specs=[pl.BlockSpec((B, D), lambda j: (0, 0)),
                  pl.BlockSpec((D, tn), lambda j: (0, j)),
                  pl.BlockSpec((1, tn), lambda j: (0, j))],
        out_specs=pl.BlockSpec((B, tn), lambda j: (0, j)),
        out_shape=jax.ShapeDtypeStruct((B, N), f32),
        name="adaln",
    )(c, w_ada, b_ada.reshape(1, N))


def _mixer_kernel(x_ref, mod_ref, gpre_ref, gpost_ref, win_ref, wdw_ref, bdw_ref, lng_ref, lnb_ref,
                  wco_ref, wpool_ref, bpool_ref, pscale_ref, wout_ref, o_ref,
                  aext_ref, ashift_ref, pext_ref, act_ref, *, ts, D, conv_k):
    s = pl.program_id(1)

    @pl.when(s == 0)
    def _():
        aext_ref[0:CONV_HIST, :] = jnp.zeros((CONV_HIST, D), f32)
        pext_ref[0:POOL_HIST, :] = jnp.zeros((POOL_HIST, D), f32)

    x = x_ref[0]
    shift, scale, gate = mod_ref[0, 0:1, :], mod_ref[0, 1:2, :], mod_ref[0, 2:3, :]
    h = x * lax.rsqrt(jnp.mean(x * x, axis=-1, keepdims=True) + EPS) * gpre_ref[...]
    hb = (h * (1.0 + scale) + shift).astype(bf16)

    a_val = jnp.dot(hb, win_ref[:, 0:D], preferred_element_type=f32)
    a_gate = jnp.dot(hb, win_ref[:, D:2 * D], preferred_element_type=f32)
    aext_ref[CONV_HIST:CONV_HIST + ts, :] = a_val * _sigmoid(a_gate)
    n_ext = ts + CONV_HIST
    a_full = aext_ref[...]
    for r in range(1, SUBLANES):
        ashift_ref[r - 1] = pltpu.roll(a_full, n_ext - r, axis=0)
    rc = CONV_ROWS
    off0 = CONV_HIST - (conv_k - 1)
    for c in range(ts // rc):
        acc = jnp.broadcast_to(bdw_ref[...], (rc, D))
        for k in range(conv_k):
            r, q = (off0 + k) % SUBLANES, (off0 + k) // SUBLANES
            row0 = c * rc + q * SUBLANES
            tap = aext_ref[row0:row0 + rc, :] if r == 0 else ashift_ref[r - 1, row0:row0 + rc, :]
            acc = acc + jnp.concatenate([wdw_ref[k]] * (rc // SUBLANES), axis=0) * tap
        mu = jnp.mean(acc, axis=-1, keepdims=True)
        cen = acc - mu
        var = jnp.mean(cen * cen, axis=-1, keepdims=True)
        ln = cen * lax.rsqrt(var + EPS) * lng_ref[...] + lnb_ref[...]
        act_ref[c * rc:(c + 1) * rc, :] = _silu(ln).astype(bf16)
    aext_ref[0:CONV_HIST, :] = aext_ref[ts:ts + CONV_HIST, :]
    a_out = jnp.dot(act_ref[...], wco_ref[...], preferred_element_type=f32)

    pext_ref[POOL_HIST:POOL_HIST + ts, :] = jnp.dot(hb, win_ref[:, 2 * D:3 * D], preferred_element_type=f32)
    t_glob = s * ts + lax.broadcasted_iota(i32, (ts, 1), 0)
    dg = D // len(POOL_WINDOWS)
    bms = []
    for g, win in enumerate(POOL_WINDOWS):
        cols = slice(g * dg, (g + 1) * dg)
        p_ext = pext_ref[:, cols]
        wsum, j = p_ext, 1
        while j < win:
            wsum = wsum + pltpu.roll(wsum, j, axis=0)
            j *= 2
        cur = p_ext[POOL_HIST:, :]
        cnt = jnp.minimum(t_glob + 1, win).astype(f32)
        pooled = wsum[POOL_HIST:, :] / cnt - cur
        bms.append(jnp.dot(pooled.astype(bf16), wpool_ref[g], preferred_element_type=f32))
    pext_ref[0:POOL_HIST, :] = pext_ref[ts:ts + POOL_HIST, :]
    bm = (jnp.concatenate(bms, axis=-1) + bpool_ref[...]) * pscale_ref[...]

    g_a = _sigmoid(jnp.dot(hb, win_ref[:, 3 * D:4 * D], preferred_element_type=f32))
    g_b = _sigmoid(jnp.dot(hb, win_ref[:, 4 * D:5 * D], preferred_element_type=f32))
    mixed = (g_a * a_out + g_b * bm).astype(bf16)
    y = jnp.dot(mixed, wout_ref[...], preferred_element_type=f32)
    yn = y * lax.rsqrt(jnp.mean(y * y, axis=-1, keepdims=True) + EPS) * gpost_ref[...]
    o_ref[0] = x + gate * yn


def _mixer(x, mod_m, g_pre, g_post, w_in, w_dw, b_dw, ln_g, ln_b, w_conv_out, w_pool, b_pool, pool_scale,
           w_out, *, ts):
    B, S, D = x.shape
    conv_k = w_dw.shape[0]
    assert conv_k - 1 <= CONV_HIST and max(POOL_WINDOWS) - 1 <= POOL_HIST
    assert S % ts == 0 and ts >= CONV_HIST and D % (LANES * len(POOL_WINDOWS)) == 0
    row = lambda v: v.reshape(1, D)
    vm = pl.BlockSpec(memory_space=pltpu.VMEM)
    kern = functools.partial(_mixer_kernel, ts=ts, D=D, conv_k=conv_k)
    return pl.pallas_call(
        kern,
        grid=(B, S // ts),
        in_specs=[pl.BlockSpec((1, ts, D), lambda b, s: (b, s, 0)),
                  pl.BlockSpec((1, 3, D), lambda b, s: (b, 0, 0)),
                  vm, vm, vm, vm, vm, vm, vm, vm, vm, vm, vm, vm],
        out_specs=pl.BlockSpec((1, ts, D), lambda b, s: (b, s, 0)),
        out_shape=jax.ShapeDtypeStruct((B, S, D), f32),
        scratch_shapes=[pltpu.VMEM((ts + CONV_HIST, D), f32),
                        pltpu.VMEM((SUBLANES - 1, ts + CONV_HIST, D), f32),
                        pltpu.VMEM((ts + POOL_HIST, D), f32),
                        pltpu.VMEM((ts, D), bf16)],
        compiler_params=pltpu.CompilerParams(dimension_semantics=("arbitrary", "arbitrary"),
                                             vmem_limit_bytes=VMEM_LIMIT),
        name="mixer",
    )(x, mod_m, row(g_pre), row(g_post), w_in.astype(bf16),
      jnp.broadcast_to(w_dw[:, None, :], (conv_k, SUBLANES, D)), row(b_dw), row(ln_g), row(ln_b),
      w_conv_out.astype(bf16), w_pool.astype(bf16), row(b_pool), row(pool_scale), w_out.astype(bf16))


def _route_kernel(x_ref, mod_ref, gpre_ref, wr_ref, bias_ref,
                  h2p_ref, eidx_ref, rank_ref, wcol_ref, cnt_ref, carry_ref, *, tt, D, E):
    i = pl.program_id(0)

    @pl.when(i == 0)
    def _():
        carry_ref[...] = jnp.zeros((E, LANES), f32)

    x = x_ref[...]
    shift, scale = mod_ref[0, 0:1, :], mod_ref[0, 1:2, :]
    h = x * lax.rsqrt(jnp.mean(x * x, axis=-1, keepdims=True) + EPS) * gpre_ref[...]
    h = h * (1.0 + scale) + shift
    h2p_ref[...] = _pack_bf16_pair(h[:, :D // 2], h[:, D // 2:])

    logits = lax.dot_general(wr_ref[...], h.astype(bf16), (((1,), (1,)), ((), ())),
                             preferred_element_type=f32)
    scores = _sigmoid(logits)
    choice = scores + bias_ref[...]
    G = N_EXPERT_GROUPS
    ge = E // G
    neg = jnp.float32(-jnp.inf)

    ch3 = choice.reshape(G, ge, tt)
    m1 = jnp.max(ch3, axis=1, keepdims=True)
    is_m1 = ch3 == m1
    n_m1 = jnp.sum(is_m1.astype(f32), axis=1, keepdims=True)
    m2 = jnp.max(jnp.where(is_m1, neg, ch3), axis=1, keepdims=True)
    gs = (m1 + jnp.where(n_m1 >= 2.0, m1, m2)).reshape(G, tt)

    gi = lax.broadcasted_iota(i32, (G, tt), 0).astype(f32)
    gsel = jnp.zeros((G, tt), f32)
    for _ in range(TOPK_GROUPS):
        gm = jnp.max(gs, axis=0, keepdims=True)
        first = jnp.min(jnp.where(gs == gm, gi, float(G)), axis=0, keepdims=True)
        hit = gi == first
        gsel = jnp.where(hit, 1.0, gsel)
        gs = jnp.where(hit, neg, gs)

    emask = jnp.broadcast_to(gsel.reshape(G, 1, tt), (G, ge, tt)).reshape(E, tt)
    masked = jnp.where(emask > 0.5, choice, neg)
    ei = lax.broadcasted_iota(i32, (E, tt), 0).astype(f32)
    sel = jnp.zeros((E, tt), f32)
    idx_rows, sc_rows = [], []
    for _ in range(TOP_K):
        mx = jnp.max(masked, axis=0, keepdims=True)
        first = jnp.min(jnp.where(masked == mx, ei, float(E)), axis=0, keepdims=True)
        hit = ei == first
        idx_rows.append(first)
        sc_rows.append(jnp.sum(jnp.where(hit, scores, 0.0), axis=0, keepdims=True))
        sel = sel + hit.astype(f32)
        masked = jnp.where(hit, neg, masked)

    upper = (lax.broadcasted_iota(i32, (tt, tt), 0) < lax.broadcasted_iota(i32, (tt, tt), 1)).astype(bf16)
    selb = sel.astype(bf16)
    excl = jnp.dot(selb, upper, preferred_element_type=f32)
    carry = carry_ref[...]
    pos = excl + jnp.concatenate([carry] * (tt // LANES), axis=1)
    carry = carry + jnp.dot(selb, jnp.ones((tt, LANES), bf16), preferred_element_type=f32)
    carry_ref[...] = carry
    cnt_ref[...] = carry

    ssum = sc_rows[0]
    for r in sc_rows[1:]:
        ssum = ssum + r
    wscale = ROUTED_SCALE / ssum
    nb = tt // LANES
    for k in range(TOP_K):
        rk = jnp.sum(jnp.where(ei == idx_rows[k], pos, 0.0), axis=0, keepdims=True).astype(i32)
        for j in range(nb):
            eidx_ref[j, k:k + 1, :] = idx_rows[k][:, j * LANES:(j + 1) * LANES].astype(i32)
            rank_ref[j, k:k + 1, :] = rk[:, j * LANES:(j + 1) * LANES]
    wstack = jnp.concatenate([r * wscale for r in sc_rows] + [jnp.zeros((LANES - TOP_K, tt), f32)], axis=0)
    for j in range(nb):
        blk = wstack[:, j * LANES:(j + 1) * LANES].T
        wcol_ref[j * LANES:(j + 1) * LANES, :] = blk[:, 0:TOP_K]


def _route(x1, mod_f, g_pre, w_router, router_bias, *, tt):
    B, S, D = x1.shape
    T = B * S
    E = w_router.shape[1]
    assert S % tt == 0 and tt % LANES == 0 and E % (N_EXPERT_GROUPS * SUBLANES) == 0
    spt = S // tt
    nb = tt // LANES
    kern = functools.partial(_route_kernel, tt=tt, D=D, E=E)
    return pl.pallas_call(
        kern,
        grid=(T // tt,),
        in_specs=[pl.BlockSpec((tt, D), lambda i: (i, 0)),
                  pl.BlockSpec((1, 2, D), lambda i: (i // spt, 0, 0)),
                  pl.BlockSpec((1, D), lambda i: (0, 0)),
                  pl.BlockSpec((E, D), lambda i: (0, 0)),
                  pl.BlockSpec((E, 1), lambda i: (0, 0))],
        out_specs=[pl.BlockSpec((tt, D // 2), lambda i: (i, 0)),
                   pl.BlockSpec((nb, TOP_K, LANES), lambda i: (i, 0, 0)),
                   pl.BlockSpec((nb, TOP_K, LANES), lambda i: (i, 0, 0)),
                   pl.BlockSpec((tt, TOP_K), lambda i: (i, 0)),
                   pl.BlockSpec((E, LANES), lambda i: (0, 0))],
        out_shape=[jax.ShapeDtypeStruct((T, D // 2), i32),
                   jax.ShapeDtypeStruct((T // LANES, TOP_K, LANES), i32),
                   jax.ShapeDtypeStruct((T // LANES, TOP_K, LANES), i32),
                   jax.ShapeDtypeStruct((T, TOP_K), f32),
                   jax.ShapeDtypeStruct((E, LANES), f32)],
        scratch_shapes=[pltpu.VMEM((E, LANES), f32)],
        compiler_params=pltpu.CompilerParams(dimension_semantics=("arbitrary",), vmem_limit_bytes=VMEM_LIMIT),
        name="route",
    )(x1.reshape(T, D), mod_f, g_pre.reshape(1, D), w_router.T.astype(bf16), router_bias.reshape(E, 1))


def _plan_kernel(cnt_ref, eidx_ref, rank_ref, dest_ref, tstart_ref, total_ref, pstart_ref, *, E, TM, nb):
    i = pl.program_id(0)

    @pl.when(i == 0)
    def _():
        cnt_col = cnt_ref[:, 0:1]
        pad_col = jnp.ceil(cnt_col / TM) * TM
        r = lax.broadcasted_iota(i32, (E, E), 0)
        c = lax.broadcasted_iota(i32, (E, E), 1)
        pad_row = jnp.sum(jnp.where(r == c, pad_col, 0.0), axis=0, keepdims=True)
        pstart = jnp.sum(jnp.where(c < r, pad_row, 0.0), axis=1, keepdims=True)
        pstart_ref[...] = jnp.broadcast_to(pstart, (E, LANES))
        pstart_row = jnp.sum(jnp.where(r == c, pstart, 0.0), axis=0, keepdims=True)
        tstart_ref[...] = (pstart_row / TM).astype(i32)
        total = jnp.sum(pad_col, axis=0, keepdims=True)
        total_ref[...] = jnp.broadcast_to(total / TM, (1, LANES)).astype(i32)

    ei = lax.broadcasted_iota(i32, (E, LANES), 0)
    pstart = pstart_ref[...]
    for j in range(nb):
        for k in range(TOP_K):
            e_row = eidx_ref[j, k:k + 1, :]
            base = jnp.sum(jnp.where(ei == e_row, pstart, 0.0), axis=0, keepdims=True).astype(i32)
            dest_ref[j, k:k + 1, :] = base + rank_ref[j, k:k + 1, :]


def _plan(cnt, eidx, rank, *, TM):
    E = cnt.shape[0]
    NB = eidx.shape[0]
    nb = 4
    assert NB % nb == 0
    kern = functools.partial(_plan_kernel, E=E, TM=TM, nb=nb)
    blk = pl.BlockSpec((nb, TOP_K, LANES), lambda i: (i, 0, 0))
    dest, tstart, total = pl.pallas_call(
        kern,
        grid=(NB // nb,),
        in_specs=[pl.BlockSpec((E, LANES), lambda i: (0, 0)), blk, blk],
        out_specs=[blk,
                   pl.BlockSpec((1, E), lambda i: (0, 0)),
                   pl.BlockSpec((1, LANES), lambda i: (0, 0))],
        out_shape=[jax.ShapeDtypeStruct((NB, TOP_K, LANES), i32),
                   jax.ShapeDtypeStruct((1, E), i32),
                   jax.ShapeDtypeStruct((1, LANES), i32)],
        scratch_shapes=[pltpu.VMEM((E, LANES), f32)],
        compiler_params=pltpu.CompilerParams(dimension_semantics=("arbitrary",)),
        name="plan",
    )(cnt, eidx, rank)
    return dest, jnp.concatenate([tstart.reshape(E), total.reshape(LANES)[0:1]])


def _sc_workers():
    info = plsc.get_sparse_core_info()
    return info.num_cores, info.num_subcores


def _dispatch(h2p, dest, NP):
    T, W = h2p.shape
    NC, NS = _sc_workers()
    nblk = T // (NC * NS * SC_ROWS)
    assert T == nblk * NC * NS * SC_ROWS
    mesh = plsc.VectorSubcoreMesh(core_axis_name="c", subcore_axis_name="s")

    @functools.partial(
        pl.kernel, mesh=mesh, out_type=jax.ShapeDtypeStruct((NP, W), h2p.dtype),
        scratch_types=[pltpu.VMEM((TOP_K, SC_ROWS), i32), pltpu.VMEM((SC_ROWS, W), h2p.dtype)],
        name="dispatch")
    def kern(h_hbm, dest_hbm, xs_hbm, idx_v, rows_v):
        wid = lax.axis_index("s") * NC + lax.axis_index("c")

        @pl.loop(0, nblk)
        def _(b):
            blk = wid * nblk + b
            pltpu.sync_copy(dest_hbm.at[blk], idx_v)
            pltpu.sync_copy(h_hbm.at[pl.ds(pl.multiple_of(blk * SC_ROWS, SC_ROWS), SC_ROWS)], rows_v)
            for k in range(TOP_K):
                pltpu.sync_copy(rows_v, xs_hbm.at[idx_v.at[k]])

    return kern(h2p, dest)


def _collect(ys, dest, t0, tc):
    NP, W = ys.shape
    NC, NS = _sc_workers()
    nblk = tc // (NC * NS * SC_ROWS)
    assert tc == nblk * NC * NS * SC_ROWS and t0 % SC_ROWS == 0
    blk0 = t0 // SC_ROWS
    mesh = plsc.VectorSubcoreMesh(core_axis_name="c", subcore_axis_name="s")

    @functools.partial(
        pl.kernel, mesh=mesh, out_type=jax.ShapeDtypeStruct((TOP_K, tc, W), ys.dtype),
        scratch_types=[pltpu.VMEM((TOP_K, SC_ROWS), i32), pltpu.VMEM((SC_ROWS, W), ys.dtype)],
        name="collect")
    def kern(ys_hbm, dest_hbm, yg_hbm, idx_v, rows_v):
        wid = lax.axis_index("s") * NC + lax.axis_index("c")

        @pl.loop(0, nblk)
        def _(b):
            blk = wid * nblk + b
            pltpu.sync_copy(dest_hbm.at[blk0 + blk], idx_v)
            for k in range(TOP_K):
                pltpu.sync_copy(ys_hbm.at[idx_v.at[k]], rows_v)
                pltpu.sync_copy(rows_v, yg_hbm.at[k, pl.ds(pl.multiple_of(blk * SC_ROWS, SC_ROWS), SC_ROWS)])

    return kern(ys, dest)


def _experts_kernel(tstart_ref, xs_hbm, wgu_ref, wd_ref, ys_hbm, xbuf, ybuf, sem_in, sem_out, wgu_bf, wd_bf,
                    *, TM, ED, D):
    e = pl.program_id(0)
    n_exp = pl.num_programs(0)
    u_lo, u_hi, n_tiles = tstart_ref[e], tstart_ref[e + 1], tstart_ref[n_exp]

    def in_copy(u, slot):
        rows = pl.ds(pl.multiple_of(u * TM, TM), TM)
        return pltpu.make_async_copy(xs_hbm.at[rows], xbuf.at[slot], sem_in.at[slot])

    def out_copy(u, slot):
        rows = pl.ds(pl.multiple_of(u * TM, TM), TM)
        return pltpu.make_async_copy(ybuf.at[slot], ys_hbm.at[rows], sem_out.at[slot])

    @pl.when(e == 0)
    def _():
        for u in range(N_IN - 1):
            @pl.when(u < n_tiles)
            def _():
                in_copy(u, u).start()

    @pl.when(u_hi > u_lo)
    def _():
        wgu_bf[...] = wgu_ref[0].astype(bf16)
        wd_bf[...] = wd_ref[0].astype(bf16)

    def tile(u, carry):
        slot = lax.rem(u, N_IN)
        in_copy(u, slot).wait()
        ahead = u + (N_IN - 1)

        @pl.when(ahead < n_tiles)
        def _():
            in_copy(ahead, lax.rem(ahead, N_IN)).start()

        oslot = lax.rem(u, N_OUT)

        @pl.when(u >= N_OUT)
        def _():
            out_copy(u - N_OUT, oslot).wait()

        sub = TM // N_SUB
        for j in range(N_SUB):
            rows = slice(j * sub, (j + 1) * sub)
            lo, hi = _unpack_bf16_pair(xbuf[slot, rows, :])
            gu = (jnp.dot(lo.astype(bf16), wgu_bf[0:D // 2, :], preferred_element_type=f32)
                  + jnp.dot(hi.astype(bf16), wgu_bf[D // 2:, :], preferred_element_type=f32))
            act = (_silu(gu[:, :ED]) * gu[:, ED:]).astype(bf16)
            y = jnp.dot(act, wd_bf[...], preferred_element_type=f32)
            ybuf[oslot, rows, :] = _pack_bf16_pair(y[:, :D // 2], y[:, D // 2:])
        out_copy(u, oslot).start()
        return carry

    lax.fori_loop(u_lo, u_hi, tile, 0)

    @pl.when(e == n_exp - 1)
    def _():
        for back in range(N_OUT, 0, -1):
            @pl.when(n_tiles >= back)
            def _():
                out_copy(n_tiles - back, lax.rem(n_tiles - back, N_OUT)).wait()


def _experts(xs, tstart, w_gate_up, w_down, *, TM):
    NP, W = xs.shape
    E, D, ED2 = w_gate_up.shape
    ED = ED2 // 2
    kern = functools.partial(_experts_kernel, TM=TM, ED=ED, D=D)
    return pl.pallas_call(
        kern,
        grid_spec=pltpu.PrefetchScalarGridSpec(
            num_scalar_prefetch=1,
            grid=(E,),
            in_specs=[pl.BlockSpec(memory_space=pl.ANY),
                      pl.BlockSpec((1, D, ED2), lambda e, ts: (e, 0, 0)),
                      pl.BlockSpec((1, ED, D), lambda e, ts: (e, 0, 0))],
            out_specs=pl.BlockSpec(memory_space=pl.ANY),
            scratch_shapes=[pltpu.VMEM((N_IN, TM, W), i32), pltpu.VMEM((N_OUT, TM, W), i32),
                            pltpu.SemaphoreType.DMA((N_IN,)), pltpu.SemaphoreType.DMA((N_OUT,)),
                            pltpu.VMEM((D, ED2), bf16), pltpu.VMEM((ED, D), bf16)]),
        out_shape=jax.ShapeDtypeStruct((NP, W), i32),
        compiler_params=pltpu.CompilerParams(dimension_semantics=("arbitrary",), vmem_limit_bytes=VMEM_LIMIT),
        name="experts",
    )(tstart, xs, w_gate_up, w_down)


def _final_kernel(yg_ref, wcol_ref, h2p_ref, x1_ref, mod_ref, gpost_ref, wsgu_ref, wsd_ref, o_ref, *, D, SD):
    wcol = wcol_ref[...]
    r_lo = jnp.zeros(h2p_ref.shape, f32)
    r_hi = jnp.zeros(h2p_ref.shape, f32)
    for k in range(TOP_K):
        lo, hi = _unpack_bf16_pair(yg_ref[k])
        wk = wcol[:, k:k + 1]
        r_lo = r_lo + wk * lo
        r_hi = r_hi + wk * hi
    routed = jnp.concatenate([r_lo, r_hi], axis=-1)

    hlo, hhi = _unpack_bf16_pair(h2p_ref[...])
    gs = (jnp.dot(hlo.astype(bf16), wsgu_ref[0:D // 2, :], preferred_element_type=f32)
          + jnp.dot(hhi.astype(bf16), wsgu_ref[D // 2:, :], preferred_element_type=f32))
    sh = (_silu(gs[:, :SD]) * gs[:, SD:]).astype(bf16)
    y = routed + jnp.dot(sh, wsd_ref[...], preferred_element_type=f32)
    yn = y * lax.rsqrt(jnp.mean(y * y, axis=-1, keepdims=True) + EPS) * gpost_ref[...]
    o_ref[...] = x1_ref[...] + mod_ref[0, 2:3, :] * yn


def _final_kernel_chained(yg_ref, wcol_ref, h2p_ref, x1_ref, mod_ref, gpost_ref, wsgu_ref, wsd_ref, prev_ref, o_ref,
                          *, D, SD):
    del prev_ref
    _final_kernel(yg_ref, wcol_ref, h2p_ref, x1_ref, mod_ref, gpost_ref, wsgu_ref, wsd_ref, o_ref, D=D, SD=SD)


def _final(yg, wcol, h2p, x1, mod_f, g_post, wsgu, wsd, out_prev, *, t0, tt, S):
    T, D = x1.shape
    tc = yg.shape[1]
    SD = wsd.shape[0]
    spt = S // tt
    i0 = t0 // tt
    assert t0 % tt == 0 and tc % tt == 0
    kern = functools.partial(_final_kernel, D=D, SD=SD)
    in_specs = [pl.BlockSpec((TOP_K, tt, D // 2), lambda i: (0, i, 0)),
                pl.BlockSpec((tt, TOP_K), lambda i: (i0 + i, 0)),
                pl.BlockSpec((tt, D // 2), lambda i: (i0 + i, 0)),
                pl.BlockSpec((tt, D), lambda i: (i0 + i, 0)),
                pl.BlockSpec((1, 3, D), lambda i: ((i0 + i) // spt, 0, 0)),
                pl.BlockSpec((1, D), lambda i: (0, 0)),
                pl.BlockSpec((D, 2 * SD), lambda i: (0, 0)),
                pl.BlockSpec((SD, D), lambda i: (0, 0))]
    args = [yg, wcol, h2p, x1, mod_f, g_post.reshape(1, D), wsgu, wsd]
    aliases = {}
    if out_prev is not None:
        in_specs.append(pl.BlockSpec(memory_space=pl.ANY))
        args.append(out_prev)
        aliases = {len(args) - 1: 0}
        kern = functools.partial(_final_kernel_chained, D=D, SD=SD)
    return pl.pallas_call(
        kern,
        grid=(tc // tt,),
        in_specs=in_specs,
        out_specs=pl.BlockSpec((tt, D), lambda i: (i0 + i, 0)),
        out_shape=jax.ShapeDtypeStruct((T, D), f32),
        input_output_aliases=aliases,
        compiler_params=pltpu.CompilerParams(dimension_semantics=("arbitrary",), vmem_limit_bytes=VMEM_LIMIT),
        name="final",
    )(*args)


def _layer(x, mod, g_pre_mix, g_post_mix, g_pre_ffn, g_post_ffn, w_in, w_dw, b_dw, ln_g, ln_b, w_conv_out,
           w_pool, b_pool, pool_scale, w_out, w_router, router_bias, w_gate_up, w_down, ws_gate_up, ws_down):
    B, S, D = x.shape
    T = B * S
    E = w_router.shape[1]
    mod6 = mod.reshape(B, 6, D)
    mod_m, mod_f = mod6[:, 0:3], mod6[:, 3:6]
    x1 = _mixer(x, mod_m, g_pre_mix, g_post_mix, w_in, w_dw, b_dw, ln_g, ln_b, w_conv_out, w_pool, b_pool,
                pool_scale, w_out, ts=min(256, S))
    h2p, eidx, rank, wcol, cnt = _route(x1, mod_f[:, 0:2], g_pre_ffn, w_router, router_bias, tt=min(512, S))
    TM = 512
    NT = (T * TOP_K) // TM + E
    dest, tstart = _plan(cnt, eidx, rank, TM=TM)
    xs = _dispatch(h2p, dest, NT * TM)
    ys = _experts(xs, tstart, w_gate_up, w_down, TM=TM)
    x1f = x1.reshape(T, D)
    wsgu, wsd = ws_gate_up.astype(bf16), ws_down.astype(bf16)
    tc = T // N_TAIL_CHUNKS
    out = None
    for ci in range(N_TAIL_CHUNKS):
        yg = _collect(ys, dest, ci * tc, tc)
        out = _final(yg, wcol, h2p, x1f, mod_f, g_post_ffn, wsgu, wsd, out, t0=ci * tc, tt=min(256, S), S=S)
    return out.reshape(B, S, D)


def kernel(x, c, w_ada, b_ada, g_pre_mix, g_post_mix, g_pre_ffn, g_post_ffn, w_in, w_dw, b_dw, ln_g, ln_b,
           w_conv_out, w_pool, b_pool, pool_scale, w_out, w_router, router_bias, w_gate_up, w_down, ws_gate_up,
           ws_down):
    depth = w_ada.shape[0]
    for l in range(depth):
        mod = _adaln(c, w_ada[l], b_ada[l])
        x = _layer(x, mod, g_pre_mix[l], g_post_mix[l], g_pre_ffn[l], g_post_ffn[l], w_in[l], w_dw[l], b_dw[l],
                   ln_g[l], ln_b[l], w_conv_out[l], w_pool[l], b_pool[l], pool_scale[l], w_out[l], w_router[l],
                   router_bias[l], w_gate_up[l], w_down[l], ws_gate_up[l], ws_down[l])
    return x
```

```python
import functools

import jax
import jax.numpy as jnp
from jax import lax
from jax.experimental import pallas as pl
from jax.experimental.pallas import tpu as pltpu
from jax.experimental.pallas import tpu_sc as plsc

EPS = 1e-6
TOP_K = 8
N_EXPERT_GROUPS = 8
TOPK_GROUPS = 4
ROUTED_SCALE = 2.5
POOL_WINDOWS = (2, 4, 8, 16)

LANES = 128
SUBLANES = 8
CONV_HIST = 32
POOL_HIST = 16
CONV_ROWS = 32
N_IN = 4
N_OUT = 2
N_SUB = 2
N_TAIL_CHUNKS = 4
SC_ROWS = 128
VMEM_LIMIT = 56 * 1024 * 1024

f32 = jnp.float32
bf16 = jnp.bfloat16
i32 = jnp.int32


def _sigmoid(v):
    return 0.5 * jnp.tanh(0.5 * v) + 0.5


def _silu(v):
    return v * _sigmoid(v)


def _pack_bf16_pair(lo, hi):
    lo_bits = lax.bitcast_convert_type(lo.astype(bf16).astype(f32), jnp.uint32)
    hi_bits = lax.bitcast_convert_type(hi.astype(bf16).astype(f32), jnp.uint32)
    word = lax.shift_right_logical(lo_bits, jnp.uint32(16)) | hi_bits
    return lax.bitcast_convert_type(word, i32)


def _unpack_bf16_pair(word):
    w = lax.bitcast_convert_type(word, jnp.uint32)
    lo = lax.bitcast_convert_type(lax.shift_left(w, jnp.uint32(16)), f32)
    hi = lax.bitcast_convert_type(w & jnp.uint32(0xFFFF0000), f32)
    return lo, hi


def _adaln_kernel(c_ref, w_ref, b_ref, o_ref):
    cond = _silu(c_ref[...])
    o_ref[...] = jnp.dot(cond.astype(bf16), w_ref[...].astype(bf16), preferred_element_type=f32) + b_ref[...]


def _adaln(c, w_ada, b_ada):
    B, D = c.shape
    N = w_ada.shape[1]
    tn = 1024
    return pl.pallas_call(
        _adaln_kernel,
        grid=(N // tn,),
        in_specs=[pl.BlockSpec((B, D), lambda j: (0, 0)),
                  pl.BlockSpec((D, tn), lambda j: (0, j)),
                  pl.BlockSpec((1, tn), lambda j: (0, j))],
        out_specs=pl.BlockSpec((B, tn), lambda j: (0, j)),
        out_shape=jax.ShapeDtypeStruct((B, N), f32),
        name="adaln",
    )(c, w_ada, b_ada.reshape(1, N))


def _mixer_kernel(x_ref, mod_ref, gpre_ref, gpost_ref, win_ref, wdw_ref, bdw_ref, lng_ref, lnb_ref,
                  wco_ref, wpool_ref, bpool_ref, pscale_ref, wout_ref, o_ref,
                  aext_ref, ashift_ref, pext_ref, act_ref, *, ts, D, conv_k):
    s = pl.program_id(1)

    @pl.when(s == 0)
    def _():
        aext_ref[0:CONV_HIST, :] = jnp.zeros((CONV_HIST, D), f32)
        pext_ref[0:POOL_HIST, :] = jnp.zeros((POOL_HIST, D), f32)

    x = x_ref[0]
    shift, scale, gate = mod_ref[0, 0:1, :], mod_ref[0, 1:2, :], mod_ref[0, 2:3, :]
    h = x * lax.rsqrt(jnp.mean(x * x, axis=-1, keepdims=True) + EPS) * gpre_ref[...]
    hb = (h * (1.0 + scale) + shift).astype(bf16)

    a_val = jnp.dot(hb, win_ref[:, 0:D], preferred_element_type=f32)
    a_gate = jnp.dot(hb, win_ref[:, D:2 * D], preferred_element_type=f32)
    aext_ref[CONV_HIST:CONV_HIST + ts, :] = a_val * _sigmoid(a_gate)
    n_ext = ts + CONV_HIST
    a_full = aext_ref[...]
    for r in range(1, SUBLANES):
        ashift_ref[r - 1] = pltpu.roll(a_full, n_ext - r, axis=0)
    rc = CONV_ROWS
    off0 = CONV_HIST - (conv_k - 1)
    for c in range(ts // rc):
        acc = jnp.broadcast_to(bdw_ref[...], (rc, D))
        for k in range(conv_k):
            r, q = (off0 + k) % SUBLANES, (off0 + k) // SUBLANES
            row0 = c * rc + q * SUBLANES
            tap = aext_ref[row0:row0 + rc, :] if r == 0 else ashift_ref[r - 1, row0:row0 + rc, :]
            acc = acc + jnp.concatenate([wdw_ref[k]] * (rc // SUBLANES), axis=0) * tap
        mu = jnp.mean(acc, axis=-1, keepdims=True)
        cen = acc - mu
        var = jnp.mean(cen * cen, axis=-1, keepdims=True)
        ln = cen * lax.rsqrt(var + EPS) * lng_ref[...] + lnb_ref[...]
        act_ref[c * rc:(c + 1) * rc, :] = _silu(ln).astype(bf16)
    aext_ref[0:CONV_HIST, :] = aext_ref[ts:ts + CONV_HIST, :]
    a_out = jnp.dot(act_ref[...], wco_ref[...], preferred_element_type=f32)

    pext_ref[POOL_HIST:POOL_HIST + ts, :] = jnp.dot(hb, win_ref[:, 2 * D:3 * D], preferred_element_type=f32)
    t_glob = s * ts + lax.broadcasted_iota(i32, (ts, 1), 0)
    dg = D // len(POOL_WINDOWS)
    bms = []
    for g, win in enumerate(POOL_WINDOWS):
        cols = slice(g * dg, (g + 1) * dg)
        p_ext = pext_ref[:, cols]
        wsum, j = p_ext, 1
        while j < win:
            wsum = wsum + pltpu.roll(wsum, j, axis=0)
            j *= 2
        cur = p_ext[POOL_HIST:, :]
        cnt = jnp.minimum(t_glob + 1, win).astype(f32)
        pooled = wsum[POOL_HIST:, :] / cnt - cur
        bms.append(jnp.dot(pooled.astype(bf16), wpool_ref[g], preferred_element_type=f32))
    pext_ref[0:POOL_HIST, :] = pext_ref[ts:ts + POOL_HIST, :]
    bm = (jnp.concatenate(bms, axis=-1) + bpool_ref[...]) * pscale_ref[...]

    g_a = _sigmoid(jnp.dot(hb, win_ref[:, 3 * D:4 * D], preferred_element_type=f32))
    g_b = _sigmoid(jnp.dot(hb, win_ref[:, 4 * D:5 * D], preferred_element_type=f32))
    mixed = (g_a * a_out + g_b * bm).astype(bf16)
    y = jnp.dot(mixed, wout_ref[...], preferred_element_type=f32)
    yn = y * lax.rsqrt(jnp.mean(y * y, axis=-1, keepdims=True) + EPS) * gpost_ref[...]
    o_ref[0] = x + gate * yn


def _mixer(x, mod_m, g_pre, g_post, w_in, w_dw, b_dw, ln_g, ln_b, w_conv_out, w_pool, b_pool, pool_scale,
           w_out, *, ts):
    B, S, D = x.shape
    conv_k = w_dw.shape[0]
    assert conv_k - 1 <= CONV_HIST and max(POOL_WINDOWS) - 1 <= POOL_HIST
    assert S % ts == 0 and ts >= CONV_HIST and D % (LANES * len(POOL_WINDOWS)) == 0
    row = lambda v: v.reshape(1, D)
    vm = pl.BlockSpec(memory_space=pltpu.VMEM)
    kern = functools.partial(_mixer_kernel, ts=ts, D=D, conv_k=conv_k)
    return pl.pallas_call(
        kern,
        grid=(B, S // ts),
        in_specs=[pl.BlockSpec((1, ts, D), lambda b, s: (b, s, 0)),
                  pl.BlockSpec((1, 3, D), lambda b, s: (b, 0, 0)),
                  vm, vm, vm, vm, vm, vm, vm, vm, vm, vm, vm, vm],
        out_specs=pl.BlockSpec((1, ts, D), lambda b, s: (b, s, 0)),
        out_shape=jax.ShapeDtypeStruct((B, S, D), f32),
        scratch_shapes=[pltpu.VMEM((ts + CONV_HIST, D), f32),
                        pltpu.VMEM((SUBLANES - 1, ts + CONV_HIST, D), f32),
                        pltpu.VMEM((ts + POOL_HIST, D), f32),
                        pltpu.VMEM((ts, D), bf16)],
        compiler_params=pltpu.CompilerParams(dimension_semantics=("arbitrary", "arbitrary"),
                                             vmem_limit_bytes=VMEM_LIMIT),
        name="mixer",
    )(x, mod_m, row(g_pre), row(g_post), w_in.astype(bf16),
      jnp.broadcast_to(w_dw[:, None, :], (conv_k, SUBLANES, D)), row(b_dw), row(ln_g), row(ln_b),
      w_conv_out.astype(bf16), w_pool.astype(bf16), row(b_pool), row(pool_scale), w_out.astype(bf16))


def _route_kernel(x_ref, mod_ref, gpre_ref, wr_ref, bias_ref,
                  h2p_ref, eidx_ref, rank_ref, wcol_ref, cnt_ref, carry_ref, *, tt, D, E):
    i = pl.program_id(0)

    @pl.when(i == 0)
    def _():
        carry_ref[...] = jnp.zeros((E, LANES), f32)

    x = x_ref[...]
    shift, scale = mod_ref[0, 0:1, :], mod_ref[0, 1:2, :]
    h = x * lax.rsqrt(jnp.mean(x * x, axis=-1, keepdims=True) + EPS) * gpre_ref[...]
    h = h * (1.0 + scale) + shift
    h2p_ref[...] = _pack_bf16_pair(h[:, :D // 2], h[:, D // 2:])

    logits = lax.dot_general(wr_ref[...], h.astype(bf16), (((1,), (1,)), ((), ())),
                             preferred_element_type=f32)
    scores = _sigmoid(logits)
    choice = scores + bias_ref[...]
    G = N_EXPERT_GROUPS
    ge = E // G
    neg = jnp.float32(-jnp.inf)

    ch3 = choice.reshape(G, ge, tt)
    m1 = jnp.max(ch3, axis=1, keepdims=True)
    is_m1 = ch3 == m1
    n_m1 = jnp.sum(is_m1.astype(f32), axis=1, keepdims=True)
    m2 = jnp.max(jnp.where(is_m1, neg, ch3), axis=1, keepdims=True)
    gs = (m1 + jnp.where(n_m1 >= 2.0, m1, m2)).reshape(G, tt)

    gi = lax.broadcasted_iota(i32, (G, tt), 0).astype(f32)
    gsel = jnp.zeros((G, tt), f32)
    for _ in range(TOPK_GROUPS):
        gm = jnp.max(gs, axis=0, keepdims=True)
        first = jnp.min(jnp.where(gs == gm, gi, float(G)), axis=0, keepdims=True)
        hit = gi == first
        gsel = jnp.where(hit, 1.0, gsel)
        gs = jnp.where(hit, neg, gs)

    emask = jnp.broadcast_to(gsel.reshape(G, 1, tt), (G, ge, tt)).reshape(E, tt)
    masked = jnp.where(emask > 0.5, choice, neg)
    ei = lax.broadcasted_iota(i32, (E, tt), 0).astype(f32)
    sel = jnp.zeros((E, tt), f32)
    idx_rows, sc_rows = [], []
    for _ in range(TOP_K):
        mx = jnp.max(masked, axis=0, keepdims=True)
        first = jnp.min(jnp.where(masked == mx, ei, float(E)), axis=0, keepdims=True)
        hit = ei == first
        idx_rows.append(first)
        sc_rows.append(jnp.sum(jnp.where(hit, scores, 0.0), axis=0, keepdims=True))
        sel = sel + hit.astype(f32)
        masked = jnp.where(hit, neg, masked)

    upper = (lax.broadcasted_iota(i32, (tt, tt), 0) < lax.broadcasted_iota(i32, (tt, tt), 1)).astype(bf16)
    selb = sel.astype(bf16)
    excl = jnp.dot(selb, upper, preferred_element_type=f32)
    carry = carry_ref[...]
    pos = excl + jnp.concatenate([carry] * (tt // LANES), axis=1)
    carry = carry + jnp.dot(selb, jnp.ones((tt, LANES), bf16), preferred_element_type=f32)
    carry_ref[...] = carry
    cnt_ref[...] = carry

    ssum = sc_rows[0]
    for r in sc_rows[1:]:
        ssum = ssum + r
    wscale = ROUTED_SCALE / ssum
    nb = tt // LANES
    for k in range(TOP_K):
        rk = jnp.sum(jnp.where(ei == idx_rows[k], pos, 0.0), axis=0, keepdims=True).astype(i32)
        for j in range(nb):
            eidx_ref[j, k:k + 1, :] = idx_rows[k][:, j * LANES:(j + 1) * LANES].astype(i32)
            rank_ref[j, k:k + 1, :] = rk[:, j * LANES:(j + 1) * LANES]
    wstack = jnp.concatenate([r * wscale for r in sc_rows] + [jnp.zeros((LANES - TOP_K, tt), f32)], axis=0)
    for j in range(nb):
        blk = wstack[:, j * LANES:(j + 1) * LANES].T
        wcol_ref[j * LANES:(j + 1) * LANES, :] = blk[:, 0:TOP_K]


def _route(x1, mod_f, g_pre, w_router, router_bias, *, tt):
    B, S, D = x1.shape
    T = B * S
    E = w_router.shape[1]
    assert S % tt == 0 and tt % LANES == 0 and E % (N_EXPERT_GROUPS * SUBLANES) == 0
    spt = S // tt
    nb = tt // LANES
    kern = functools.partial(_route_kernel, tt=tt, D=D, E=E)
    return pl.pallas_call(
        kern,
        grid=(T // tt,),
        in_specs=[pl.BlockSpec((tt, D), lambda i: (i, 0)),
                  pl.BlockSpec((1, 2, D), lambda i: (i // spt, 0, 0)),
                  pl.BlockSpec((1, D), lambda i: (0, 0)),
                  pl.BlockSpec((E, D), lambda i: (0, 0)),
                  pl.BlockSpec((E, 1), lambda i: (0, 0))],
        out_specs=[pl.BlockSpec((tt, D // 2), lambda i: (i, 0)),
                   pl.BlockSpec((nb, TOP_K, LANES), lambda i: (i, 0, 0)),
                   pl.BlockSpec((nb, TOP_K, LANES), lambda i: (i, 0, 0)),
                   pl.BlockSpec((tt, TOP_K), lambda i: (i, 0)),
                   pl.BlockSpec((E, LANES), lambda i: (0, 0))],
        out_shape=[jax.ShapeDtypeStruct((T, D // 2), i32),
                   jax.ShapeDtypeStruct((T // LANES, TOP_K, LANES), i32),
                   jax.ShapeDtypeStruct((T // LANES, TOP_K, LANES), i32),
                   jax.ShapeDtypeStruct((T, TOP_K), f32),
                   jax.ShapeDtypeStruct((E, LANES), f32)],
        scratch_shapes=[pltpu.VMEM((E, LANES), f32)],
        compiler_params=pltpu.CompilerParams(dimension_semantics=("arbitrary",), vmem_limit_bytes=VMEM_LIMIT),
        name="route",
    )(x1.reshape(T, D), mod_f, g_pre.reshape(1, D), w_router.T.astype(bf16), router_bias.reshape(E, 1))


def _plan_kernel(cnt_ref, eidx_ref, rank_ref, dest_ref, tstart_ref, total_ref, pstart_ref, *, E, TM, nb):
    i = pl.program_id(0)

    @pl.when(i == 0)
    def _():
        cnt_col = cnt_ref[:, 0:1]
        pad_col = jnp.ceil(cnt_col / TM) * TM
        r = lax.broadcasted_iota(i32, (E, E), 0)
        c = lax.broadcasted_iota(i32, (E, E), 1)
        pad_row = jnp.sum(jnp.where(r == c, pad_col, 0.0), axis=0, keepdims=True)
        pstart = jnp.sum(jnp.where(c < r, pad_row, 0.0), axis=1, keepdims=True)
        pstart_ref[...] = jnp.broadcast_to(pstart, (E, LANES))
        pstart_row = jnp.sum(jnp.where(r == c, pstart, 0.0), axis=0, keepdims=True)
        tstart_ref[...] = (pstart_row / TM).astype(i32)
        total = jnp.sum(pad_col, axis=0, keepdims=True)
        total_ref[...] = jnp.broadcast_to(total / TM, (1, LANES)).astype(i32)

    ei = lax.broadcasted_iota(i32, (E, LANES), 0)
    pstart = pstart_ref[...]
    for j in range(nb):
        for k in range(TOP_K):
            e_row = eidx_ref[j, k:k + 1, :]
            base = jnp.sum(jnp.where(ei == e_row, pstart, 0.0), axis=0, keepdims=True).astype(i32)
            dest_ref[j, k:k + 1, :] = base + rank_ref[j, k:k + 1, :]


def _plan(cnt, eidx, rank, *, TM):
    E = cnt.shape[0]
    NB = eidx.shape[0]
    nb = min(16, NB)
    assert NB % nb == 0
    kern = functools.partial(_plan_kernel, E=E, TM=TM, nb=nb)
    blk = pl.BlockSpec((nb, TOP_K, LANES), lambda i: (i, 0, 0))
    dest, tstart, total = pl.pallas_call(
        kern,
        grid=(NB // nb,),
        in_specs=[pl.BlockSpec((E, LANES), lambda i: (0, 0)), blk, blk],
        out_specs=[blk,
                   pl.BlockSpec((1, E), lambda i: (0, 0)),
                   pl.BlockSpec((1, LANES), lambda i: (0, 0))],
        out_shape=[jax.ShapeDtypeStruct((NB, TOP_K, LANES), i32),
                   jax.ShapeDtypeStruct((1, E), i32),
                   jax.ShapeDtypeStruct((1, LANES), i32)],
        scratch_shapes=[pltpu.VMEM((E, LANES), f32)],
        compiler_params=pltpu.CompilerParams(dimension_semantics=("arbitrary",)),
        name="plan",
    )(cnt, eidx, rank)
    return dest, jnp.concatenate([tstart.reshape(E), total.reshape(LANES)[0:1]])


def _sc_workers():
    info = plsc.get_sparse_core_info()
    return info.num_cores, info.num_subcores


def _dispatch(h2p, dest, NP):
    T, W = h2p.shape
    NC, NS = _sc_workers()
    nblk = T // (NC * NS * SC_ROWS)
    assert T == nblk * NC * NS * SC_ROWS
    mesh = plsc.VectorSubcoreMesh(core_axis_name="c", subcore_axis_name="s")

    @functools.partial(
        pl.kernel, mesh=mesh, out_type=jax.ShapeDtypeStruct((NP, W), h2p.dtype),
        scratch_types=[pltpu.VMEM((TOP_K, SC_ROWS), i32), pltpu.VMEM((SC_ROWS, W), h2p.dtype)],
        name="dispatch")
    def kern(h_hbm, dest_hbm, xs_hbm, idx_v, rows_v):
        wid = lax.axis_index("s") * NC + lax.axis_index("c")

        @pl.loop(0, nblk)
        def _(b):
            blk = wid * nblk + b
            pltpu.sync_copy(dest_hbm.at[blk], idx_v)
            pltpu.sync_copy(h_hbm.at[pl.ds(pl.multiple_of(blk * SC_ROWS, SC_ROWS), SC_ROWS)], rows_v)
            for k in range(TOP_K):
                pltpu.sync_copy(rows_v, xs_hbm.at[idx_v.at[k]])

    return kern(h2p, dest)


def _collect(ys, dest, t0, tc):
    NP, W = ys.shape
    NC, NS = _sc_workers()
    nblk = tc // (NC * NS * SC_ROWS)
    assert tc == nblk * NC * NS * SC_ROWS and t0 % SC_ROWS == 0
    blk0 = t0 // SC_ROWS
    mesh = plsc.VectorSubcoreMesh(core_axis_name="c", subcore_axis_name="s")

    @functools.partial(
        pl.kernel, mesh=mesh, out_type=jax.ShapeDtypeStruct((TOP_K, tc, W), ys.dtype),
        scratch_types=[pltpu.VMEM((TOP_K, SC_ROWS), i32), pltpu.VMEM((SC_ROWS, W), ys.dtype)],
        name="collect")
    def kern(ys_hbm, dest_hbm, yg_hbm, idx_v, rows_v):
        wid = lax.axis_index("s") * NC + lax.axis_index("c")

        @pl.loop(0, nblk)
        def _(b):
            blk = wid * nblk + b
            pltpu.sync_copy(dest_hbm.at[blk0 + blk], idx_v)
            for k in range(TOP_K):
                pltpu.sync_copy(ys_hbm.at[idx_v.at[k]], rows_v)
                pltpu.sync_copy(rows_v, yg_hbm.at[k, pl.ds(pl.multiple_of(blk * SC_ROWS, SC_ROWS), SC_ROWS)])

    return kern(ys, dest)


def _experts_kernel(tstart_ref, xs_hbm, wgu_ref, wd_ref, ys_hbm, xbuf, ybuf, actbuf, state, sem_in, sem_out,
                    wgu_bf, wd_bf, *, TM, ED, D):
    e = pl.program_id(0)
    n_exp = pl.num_programs(0)
    u_lo, u_hi, n_tiles = tstart_ref[e], tstart_ref[e + 1], tstart_ref[n_exp]
    sub = TM // N_SUB

    def in_copy(u, slot):
        rows = pl.ds(pl.multiple_of(u * TM, TM), TM)
        return pltpu.make_async_copy(xs_hbm.at[rows], xbuf.at[slot], sem_in.at[slot])

    def out_copy(u, slot):
        rows = pl.ds(pl.multiple_of(u * TM, TM), TM)
        return pltpu.make_async_copy(ybuf.at[slot], ys_hbm.at[rows], sem_out.at[slot])

    def wait_in(u):
        in_copy(u, lax.rem(u, N_IN)).wait()

    def start_in_ahead(u):
        ahead = u + (N_IN - 1)

        @pl.when(ahead < n_tiles)
        def _():
            in_copy(ahead, lax.rem(ahead, N_IN)).start()

    def wait_out_slot(u):
        @pl.when(u >= N_OUT)
        def _():
            out_copy(u - N_OUT, lax.rem(u, N_OUT)).wait()

    def start_out(u):
        out_copy(u, lax.rem(u, N_OUT)).start()

    def up(u):
        slot, aslot = lax.rem(u, N_IN), lax.rem(u, 2)
        for j in range(N_SUB):
            rows = slice(j * sub, (j + 1) * sub)
            lo, hi = _unpack_bf16_pair(xbuf[slot, rows, :])
            gu = (jnp.dot(lo.astype(bf16), wgu_bf[0:D // 2, :], preferred_element_type=f32)
                  + jnp.dot(hi.astype(bf16), wgu_bf[D // 2:, :], preferred_element_type=f32))
            actbuf[aslot, rows, :] = (_silu(gu[:, :ED]) * gu[:, ED:]).astype(bf16)

    def down(u, wslot):
        oslot, aslot = lax.rem(u, N_OUT), lax.rem(u, 2)
        for j in range(N_SUB):
            rows = slice(j * sub, (j + 1) * sub)
            y = jnp.dot(actbuf[aslot, rows, :], wd_bf[wslot], preferred_element_type=f32)
            ybuf[oslot, rows, :] = _pack_bf16_pair(y[:, :D // 2], y[:, D // 2:])

    @pl.when(e == 0)
    def _():
        state[0] = 0
        state[1] = 0
        for u in range(N_IN - 1):
            @pl.when(u < n_tiles)
            def _():
                in_copy(u, u).start()

    @pl.when(u_hi > u_lo)
    def _():
        old_w = state[1]
        new_w = 1 - old_w

        @pl.when(state[0] == 1)
        def _():
            wait_in(u_lo)
            wait_out_slot(u_lo - 1)
            down(u_lo - 1, old_w)
            wgu_bf[...] = wgu_ref[0].astype(bf16)
            wd_bf[new_w] = wd_ref[0].astype(bf16)
            up(u_lo)
            start_out(u_lo - 1)
            start_in_ahead(u_lo)

        @pl.when(state[0] == 0)
        def _():
            wait_in(u_lo)
            wgu_bf[...] = wgu_ref[0].astype(bf16)
            wd_bf[new_w] = wd_ref[0].astype(bf16)
            up(u_lo)
            start_in_ahead(u_lo)

        def skewed(u, carry):
            wait_in(u)
            wait_out_slot(u - 1)
            down(u - 1, new_w)
            up(u)
            start_out(u - 1)
            start_in_ahead(u)
            return carry

        lax.fori_loop(u_lo + 1, u_hi, skewed, 0)
        state[0] = 1
        state[1] = new_w

    @pl.when(e == n_exp - 1)
    def _():
        @pl.when(state[0] == 1)
        def _():
            wait_out_slot(n_tiles - 1)
            down(n_tiles - 1, state[1])
            start_out(n_tiles - 1)

        for back in range(N_OUT, 0, -1):
            @pl.when(n_tiles >= back)
            def _():
                out_copy(n_tiles - back, lax.rem(n_tiles - back, N_OUT)).wait()


def _experts(xs, tstart, w_gate_up, w_down, *, TM):
    NP, W = xs.shape
    E, D, ED2 = w_gate_up.shape
    ED = ED2 // 2
    kern = functools.partial(_experts_kernel, TM=TM, ED=ED, D=D)
    return pl.pallas_call(
        kern,
        grid_spec=pltpu.PrefetchScalarGridSpec(
            num_scalar_prefetch=1,
            grid=(E,),
            in_specs=[pl.BlockSpec(memory_space=pl.ANY),
                      pl.BlockSpec((1, D, ED2), lambda e, ts: (e, 0, 0)),
                      pl.BlockSpec((1, ED, D), lambda e, ts: (e, 0, 0))],
            out_specs=pl.BlockSpec(memory_space=pl.ANY),
            scratch_shapes=[pltpu.VMEM((N_IN, TM, W), i32), pltpu.VMEM((N_OUT, TM, W), i32),
                            pltpu.VMEM((2, TM, ED), bf16), pltpu.SMEM((2,), i32),
                            pltpu.SemaphoreType.DMA((N_IN,)), pltpu.SemaphoreType.DMA((N_OUT,)),
                            pltpu.VMEM((D, ED2), bf16), pltpu.VMEM((2, ED, D), bf16)]),
        out_shape=jax.ShapeDtypeStruct((NP, W), i32),
        compiler_params=pltpu.CompilerParams(dimension_semantics=("arbitrary",), vmem_limit_bytes=VMEM_LIMIT),
        name="experts",
    )(tstart, xs, w_gate_up, w_down)


def _final_kernel(yg_ref, wcol_ref, h2p_ref, x1_ref, mod_ref, gpost_ref, wsgu_ref, wsd_ref, o_ref, *, D, SD):
    wcol = wcol_ref[...]
    r_lo = jnp.zeros(h2p_ref.shape, f32)
    r_hi = jnp.zeros(h2p_ref.shape, f32)
    for k in range(TOP_K):
        lo, hi = _unpack_bf16_pair(yg_ref[k])
        wk = wcol[:, k:k + 1]
        r_lo = r_lo + wk * lo
        r_hi = r_hi + wk * hi
    routed = jnp.concatenate([r_lo, r_hi], axis=-1)

    hlo, hhi = _unpack_bf16_pair(h2p_ref[...])
    gs = (jnp.dot(hlo.astype(bf16), wsgu_ref[0:D // 2, :], preferred_element_type=f32)
          + jnp.dot(hhi.astype(bf16), wsgu_ref[D // 2:, :], preferred_element_type=f32))
    sh = (_silu(gs[:, :SD]) * gs[:, SD:]).astype(bf16)
    y = routed + jnp.dot(sh, wsd_ref[...], preferred_element_type=f32)
    yn = y * lax.rsqrt(jnp.mean(y * y, axis=-1, keepdims=True) + EPS) * gpost_ref[...]
    o_ref[...] = x1_ref[...] + mod_ref[0, 2:3, :] * yn


def _final_kernel_chained(yg_ref, wcol_ref, h2p_ref, x1_ref, mod_ref, gpost_ref, wsgu_ref, wsd_ref, prev_ref, o_ref,
                          *, D, SD):
    del prev_ref
    _final_kernel(yg_ref, wcol_ref, h2p_ref, x1_ref, mod_ref, gpost_ref, wsgu_ref, wsd_ref, o_ref, D=D, SD=SD)


def _final(yg, wcol, h2p, x1, mod_f, g_post, wsgu, wsd, out_prev, *, t0, tt, S):
    T, D = x1.shape
    tc = yg.shape[1]
    SD = wsd.shape[0]
    spt = S // tt
    i0 = t0 // tt
    assert t0 % tt == 0 and tc % tt == 0
    kern = functools.partial(_final_kernel, D=D, SD=SD)
    in_specs = [pl.BlockSpec((TOP_K, tt, D // 2), lambda i: (0, i, 0)),
                pl.BlockSpec((tt, TOP_K), lambda i: (i0 + i, 0)),
                pl.BlockSpec((tt, D // 2), lambda i: (i0 + i, 0)),
                pl.BlockSpec((tt, D), lambda i: (i0 + i, 0)),
                pl.BlockSpec((1, 3, D), lambda i: ((i0 + i) // spt, 0, 0)),
                pl.BlockSpec((1, D), lambda i: (0, 0)),
                pl.BlockSpec((D, 2 * SD), lambda i: (0, 0)),
                pl.BlockSpec((SD, D), lambda i: (0, 0))]
    args = [yg, wcol, h2p, x1, mod_f, g_post.reshape(1, D), wsgu, wsd]
    aliases = {}
    if out_prev is not None:
        in_specs.append(pl.BlockSpec(memory_space=pl.ANY))
        args.append(out_prev)
        aliases = {len(args) - 1: 0}
        kern = functools.partial(_final_kernel_chained, D=D, SD=SD)
    return pl.pallas_call(
        kern,
        grid=(tc // tt,),
        in_specs=in_specs,
        out_specs=pl.BlockSpec((tt, D), lambda i: (i0 + i, 0)),
        out_shape=jax.ShapeDtypeStruct((T, D), f32),
        input_output_aliases=aliases,
        compiler_params=pltpu.CompilerParams(dimension_semantics=("arbitrary",), vmem_limit_bytes=VMEM_LIMIT),
        name="final",
    )(*args)


def _layer(x, mod, g_pre_mix, g_post_mix, g_pre_ffn, g_post_ffn, w_in, w_dw, b_dw, ln_g, ln_b, w_conv_out,
           w_pool, b_pool, pool_scale, w_out, w_router, router_bias, w_gate_up, w_down, ws_gate_up, ws_down):
    B, S, D = x.shape
    T = B * S
    E = w_router.shape[1]
    mod6 = mod.reshape(B, 6, D)
    mod_m, mod_f = mod6[:, 0:3], mod6[:, 3:6]
    x1 = _mixer(x, mod_m, g_pre_mix, g_post_mix, w_in, w_dw, b_dw, ln_g, ln_b, w_conv_out, w_pool, b_pool,
                pool_scale, w_out, ts=min(512, S))
    h2p, eidx, rank, wcol, cnt = _route(x1, mod_f[:, 0:2], g_pre_ffn, w_router, router_bias, tt=min(512, S))
    TM = 512
    NT = (T * TOP_K) // TM + E
    dest, tstart = _plan(cnt, eidx, rank, TM=TM)
    xs = _dispatch(h2p, dest, NT * TM)
    ys = _experts(xs, tstart, w_gate_up, w_down, TM=TM)
    x1f = x1.reshape(T, D)
    wsgu, wsd = ws_gate_up.astype(bf16), ws_down.astype(bf16)
    tc = T // N_TAIL_CHUNKS
    out = None
    for ci in range(N_TAIL_CHUNKS):
        yg = _collect(ys, dest, ci * tc, tc)
        out = _final(yg, wcol, h2p, x1f, mod_f, g_post_ffn, wsgu, wsd, out, t0=ci * tc, tt=min(256, S), S=S)
    return out.reshape(B, S, D)


def kernel(x, c, w_ada, b_ada, g_pre_mix, g_post_mix, g_pre_ffn, g_post_ffn, w_in, w_dw, b_dw, ln_g, ln_b,
           w_conv_out, w_pool, b_pool, pool_scale, w_out, w_router, router_bias, w_gate_up, w_down, ws_gate_up,
           ws_down):
    depth = w_ada.shape[0]
    for l in range(depth):
        mod = _adaln(c, w_ada[l], b_ada[l])
        x = _layer(x, mod, g_pre_mix[l], g_post_mix[l], g_pre_ffn[l], g_post_ffn[l], w_in[l], w_dw[l], b_dw[l],
                   ln_g[l], ln_b[l], w_conv_out[l], w_pool[l], b_pool[l], pool_scale[l], w_out[l], w_router[l],
                   router_bias[l], w_gate_up[l], w_down[l], ws_gate_up[l], ws_down[l])
    return x
```

```python
import functools

import jax
import jax.numpy as jnp
from jax import lax
from jax.experimental import pallas as pl
from jax.experimental.pallas import tpu as pltpu
from jax.experimental.pallas import tpu_sc as plsc

EPS = 1e-6
TOP_K = 8
N_EXPERT_GROUPS = 8
TOPK_GROUPS = 4
ROUTED_SCALE = 2.5
POOL_WINDOWS = (2, 4, 8, 16)

LANES = 128
SUBLANES = 8
CONV_HIST = 32
POOL_HIST = 16
CONV_ROWS = 32
N_IN = 4
N_OUT = 2
N_SUB = 2
N_TAIL_CHUNKS = 8
SC_ROWS = 128
VMEM_LIMIT = 56 * 1024 * 1024

f32 = jnp.float32
bf16 = jnp.bfloat16
i32 = jnp.int32


def _sigmoid(v):
    return 0.5 * jnp.tanh(0.5 * v) + 0.5


def _silu(v):
    return v * _sigmoid(v)


def _pack_bf16_pair(lo, hi):
    lo_bits = lax.bitcast_convert_type(lo.astype(bf16).astype(f32), jnp.uint32)
    hi_bits = lax.bitcast_convert_type(hi.astype(bf16).astype(f32), jnp.uint32)
    word = lax.shift_right_logical(lo_bits, jnp.uint32(16)) | hi_bits
    return lax.bitcast_convert_type(word, i32)


def _unpack_bf16_pair(word):
    w = lax.bitcast_convert_type(word, jnp.uint32)
    lo = lax.bitcast_convert_type(lax.shift_left(w, jnp.uint32(16)), f32)
    hi = lax.bitcast_convert_type(w & jnp.uint32(0xFFFF0000), f32)
    return lo, hi


def _adaln_kernel(c_ref, w_ref, b_ref, o_ref):
    cond = _silu(c_ref[...])
    o_ref[...] = jnp.dot(cond.astype(bf16), w_ref[...].astype(bf16), preferred_element_type=f32) + b_ref[...]


def _adaln(c, w_ada, b_ada):
    B, D = c.shape
    N = w_ada.shape[1]
    tn = 1024
    return pl.pallas_call(
        _adaln_kernel,
        grid=(N // tn,),
        in_specs=[pl.BlockSpec((B, D), lambda j: (0, 0)),
                  pl.BlockSpec((D, tn), lambda j: (0, j)),
                  pl.BlockSpec((1, tn), lambda j: (0, j))],
        out_specs=pl.BlockSpec((B, tn), lambda j: (0, j)),
        out_shape=jax.ShapeDtypeStruct((B, N), f32),
        name="adaln",
    )(c, w_ada, b_ada.reshape(1, N))


def _mixer_kernel(x_ref, mod_ref, gpre_ref, gpost_ref, win_ref, wdw_ref, bdw_ref, lng_ref, lnb_ref,
                  wco_ref, wpool_ref, bpool_ref, pscale_ref, wout_ref, o_ref,
                  aext_ref, ashift_ref, pext_ref, act_ref, *, ts, D, conv_k):
    s = pl.program_id(1)

    @pl.when(s == 0)
    def _():
        aext_ref[0:CONV_HIST, :] = jnp.zeros((CONV_HIST, D), f32)
        pext_ref[0:POOL_HIST, :] = jnp.zeros((POOL_HIST, D), f32)

    x = x_ref[0]
    shift, scale, gate = mod_ref[0, 0:1, :], mod_ref[0, 1:2, :], mod_ref[0, 2:3, :]
    hb = (x * lax.rsqrt(jnp.mean(x * x, axis=-1, keepdims=True) + EPS) * (gpre_ref[...] * (1.0 + scale))
          + shift).astype(bf16)

    a_val = jnp.dot(hb, win_ref[:, 0:D], preferred_element_type=f32)
    a_gate = jnp.dot(hb, win_ref[:, D:2 * D], preferred_element_type=f32)
    aext_ref[CONV_HIST:CONV_HIST + ts, :] = a_val * _sigmoid(a_gate)
    n_ext = ts + CONV_HIST
    a_full = aext_ref[...]
    for r in range(1, SUBLANES):
        ashift_ref[r - 1] = pltpu.roll(a_full, n_ext - r, axis=0)
    rc = CONV_ROWS
    off0 = CONV_HIST - (conv_k - 1)
    for c in range(ts // rc):
        acc = jnp.broadcast_to(bdw_ref[...], (rc, D))
        for k in range(conv_k):
            r, q = (off0 + k) % SUBLANES, (off0 + k) // SUBLANES
            row0 = c * rc + q * SUBLANES
            tap = aext_ref[row0:row0 + rc, :] if r == 0 else ashift_ref[r - 1, row0:row0 + rc, :]
            acc = acc + jnp.concatenate([wdw_ref[k]] * (rc // SUBLANES), axis=0) * tap
        mu = jnp.mean(acc, axis=-1, keepdims=True)
        cen = acc - mu
        var = jnp.mean(cen * cen, axis=-1, keepdims=True)
        ln = cen * lax.rsqrt(var + EPS) * lng_ref[...] + lnb_ref[...]
        act_ref[c * rc:(c + 1) * rc, :] = _silu(ln).astype(bf16)
    aext_ref[0:CONV_HIST, :] = aext_ref[ts:ts + CONV_HIST, :]
    a_out = jnp.dot(act_ref[...], wco_ref[...], preferred_element_type=f32)

    pext_ref[POOL_HIST:POOL_HIST + ts, :] = jnp.dot(hb, win_ref[:, 2 * D:3 * D], preferred_element_type=f32)
    t_glob = s * ts + lax.broadcasted_iota(i32, (ts, 1), 0)
    dg = D // len(POOL_WINDOWS)
    bms = []
    for g, win in enumerate(POOL_WINDOWS):
        cols = slice(g * dg, (g + 1) * dg)
        p_ext = pext_ref[:, cols]
        wsum, j = p_ext, 1
        while j < win:
            wsum = wsum + pltpu.roll(wsum, j, axis=0)
            j *= 2
        cur = p_ext[POOL_HIST:, :]
        cnt = jnp.minimum(t_glob + 1, win).astype(f32)
        pooled = wsum[POOL_HIST:, :] / cnt - cur
        bms.append(jnp.dot(pooled.astype(bf16), wpool_ref[g], preferred_element_type=f32))
    pext_ref[0:POOL_HIST, :] = pext_ref[ts:ts + POOL_HIST, :]
    bm = (jnp.concatenate(bms, axis=-1) + bpool_ref[...]) * pscale_ref[...]

    g_a = _sigmoid(jnp.dot(hb, win_ref[:, 3 * D:4 * D], preferred_element_type=f32))
    g_b = _sigmoid(jnp.dot(hb, win_ref[:, 4 * D:5 * D], preferred_element_type=f32))
    mixed = (g_a * a_out + g_b * bm).astype(bf16)
    y = jnp.dot(mixed, wout_ref[...], preferred_element_type=f32)
    o_ref[0] = x + y * lax.rsqrt(jnp.mean(y * y, axis=-1, keepdims=True) + EPS) * (gpost_ref[...] * gate)


def _mixer(x, mod_m, g_pre, g_post, w_in, w_dw, b_dw, ln_g, ln_b, w_conv_out, w_pool, b_pool, pool_scale,
           w_out, *, ts):
    B, S, D = x.shape
    conv_k = w_dw.shape[0]
    assert conv_k - 1 <= CONV_HIST and max(POOL_WINDOWS) - 1 <= POOL_HIST
    assert S % ts == 0 and ts >= CONV_HIST and D % (LANES * len(POOL_WINDOWS)) == 0
    row = lambda v: v.reshape(1, D)
    vm = pl.BlockSpec(memory_space=pltpu.VMEM)
    kern = functools.partial(_mixer_kernel, ts=ts, D=D, conv_k=conv_k)
    return pl.pallas_call(
        kern,
        grid=(B, S // ts),
        in_specs=[pl.BlockSpec((1, ts, D), lambda b, s: (b, s, 0)),
                  pl.BlockSpec((1, 3, D), lambda b, s: (b, 0, 0)),
                  vm, vm, vm, vm, vm, vm, vm, vm, vm, vm, vm, vm],
        out_specs=pl.BlockSpec((1, ts, D), lambda b, s: (b, s, 0)),
        out_shape=jax.ShapeDtypeStruct((B, S, D), f32),
        scratch_shapes=[pltpu.VMEM((ts + CONV_HIST, D), f32),
                        pltpu.VMEM((SUBLANES - 1, ts + CONV_HIST, D), f32),
                        pltpu.VMEM((ts + POOL_HIST, D), f32),
                        pltpu.VMEM((ts, D), bf16)],
        compiler_params=pltpu.CompilerParams(dimension_semantics=("arbitrary", "arbitrary"),
                                             vmem_limit_bytes=VMEM_LIMIT),
        name="mixer",
    )(x, mod_m, row(g_pre), row(g_post), w_in.astype(bf16),
      jnp.broadcast_to(w_dw[:, None, :], (conv_k, SUBLANES, D)), row(b_dw), row(ln_g), row(ln_b),
      w_conv_out.astype(bf16), w_pool.astype(bf16), row(b_pool), row(pool_scale), w_out.astype(bf16))


def _route_kernel(x_ref, mod_ref, gpre_ref, wr_ref, bias_ref,
                  h2p_ref, eidx_ref, rank_ref, wcol_ref, cnt_ref, carry_ref, *, tt, D, E):
    i = pl.program_id(0)

    @pl.when(i == 0)
    def _():
        carry_ref[...] = jnp.zeros((E, LANES), f32)

    x = x_ref[...]
    shift, scale = mod_ref[0, 0:1, :], mod_ref[0, 1:2, :]
    h = x * lax.rsqrt(jnp.mean(x * x, axis=-1, keepdims=True) + EPS) * gpre_ref[...]
    h = h * (1.0 + scale) + shift
    h2p_ref[...] = _pack_bf16_pair(h[:, :D // 2], h[:, D // 2:])

    logits = lax.dot_general(wr_ref[...], h.astype(bf16), (((1,), (1,)), ((), ())),
                             preferred_element_type=f32)
    scores = _sigmoid(logits)
    choice = scores + bias_ref[...]
    G = N_EXPERT_GROUPS
    ge = E // G
    neg = jnp.float32(-jnp.inf)

    ch3 = choice.reshape(G, ge, tt)
    m1 = jnp.max(ch3, axis=1, keepdims=True)
    is_m1 = ch3 == m1
    n_m1 = jnp.sum(is_m1.astype(f32), axis=1, keepdims=True)
    m2 = jnp.max(jnp.where(is_m1, neg, ch3), axis=1, keepdims=True)
    gs = (m1 + jnp.where(n_m1 >= 2.0, m1, m2)).reshape(G, tt)

    gi = lax.broadcasted_iota(i32, (G, tt), 0).astype(f32)
    gsel = jnp.zeros((G, tt), f32)
    for _ in range(TOPK_GROUPS):
        gm = jnp.max(gs, axis=0, keepdims=True)
        first = jnp.min(jnp.where(gs == gm, gi, float(G)), axis=0, keepdims=True)
        hit = gi == first
        gsel = jnp.where(hit, 1.0, gsel)
        gs = jnp.where(hit, neg, gs)

    emask = jnp.broadcast_to(gsel.reshape(G, 1, tt), (G, ge, tt)).reshape(E, tt)
    masked = jnp.where(emask > 0.5, choice, neg)
    ei = lax.broadcasted_iota(i32, (E, tt), 0).astype(f32)
    sel = jnp.zeros((E, tt), f32)
    idx_rows, sc_rows = [], []
    for _ in range(TOP_K):
        mx = jnp.max(masked, axis=0, keepdims=True)
        first = jnp.min(jnp.where(masked == mx, ei, float(E)), axis=0, keepdims=True)
        hit = ei == first
        idx_rows.append(first)
        sc_rows.append(jnp.sum(jnp.where(hit, scores, 0.0), axis=0, keepdims=True))
        sel = sel + hit.astype(f32)
        masked = jnp.where(hit, neg, masked)

    upper = (lax.broadcasted_iota(i32, (tt, tt), 0) < lax.broadcasted_iota(i32, (tt, tt), 1)).astype(bf16)
    selb = sel.astype(bf16)
    excl = jnp.dot(selb, upper, preferred_element_type=f32)
    carry = carry_ref[...]
    pos = excl + jnp.concatenate([carry] * (tt // LANES), axis=1)
    carry = carry + jnp.dot(selb, jnp.ones((tt, LANES), bf16), preferred_element_type=f32)
    carry_ref[...] = carry
    cnt_ref[...] = carry

    ssum = sc_rows[0]
    for r in sc_rows[1:]:
        ssum = ssum + r
    wscale = ROUTED_SCALE / ssum
    nb = tt // LANES
    for k in range(TOP_K):
        rk = jnp.sum(jnp.where(ei == idx_rows[k], pos, 0.0), axis=0, keepdims=True).astype(i32)
        for j in range(nb):
            eidx_ref[j, k:k + 1, :] = idx_rows[k][:, j * LANES:(j + 1) * LANES].astype(i32)
            rank_ref[j, k:k + 1, :] = rk[:, j * LANES:(j + 1) * LANES]
    wstack = jnp.concatenate([r * wscale for r in sc_rows] + [jnp.zeros((LANES - TOP_K, tt), f32)], axis=0)
    for j in range(nb):
        blk = wstack[:, j * LANES:(j + 1) * LANES].T
        wcol_ref[j * LANES:(j + 1) * LANES, :] = blk[:, 0:TOP_K]


def _route(x1, mod_f, g_pre, w_router, router_bias, *, tt):
    B, S, D = x1.shape
    T = B * S
    E = w_router.shape[1]
    assert S % tt == 0 and tt % LANES == 0 and E % (N_EXPERT_GROUPS * SUBLANES) == 0
    spt = S // tt
    nb = tt // LANES
    kern = functools.partial(_route_kernel, tt=tt, D=D, E=E)
    return pl.pallas_call(
        kern,
        grid=(T // tt,),
        in_specs=[pl.BlockSpec((tt, D), lambda i: (i, 0)),
                  pl.BlockSpec((1, 2, D), lambda i: (i // spt, 0, 0)),
                  pl.BlockSpec((1, D), lambda i: (0, 0)),
                  pl.BlockSpec((E, D), lambda i: (0, 0)),
                  pl.BlockSpec((E, 1), lambda i: (0, 0))],
        out_specs=[pl.BlockSpec((tt, D // 2), lambda i: (i, 0)),
                   pl.BlockSpec((nb, TOP_K, LANES), lambda i: (i, 0, 0)),
                   pl.BlockSpec((nb, TOP_K, LANES), lambda i: (i, 0, 0)),
                   pl.BlockSpec((tt, TOP_K), lambda i: (i, 0)),
                   pl.BlockSpec((E, LANES), lambda i: (0, 0))],
        out_shape=[jax.ShapeDtypeStruct((T, D // 2), i32),
                   jax.ShapeDtypeStruct((T // LANES, TOP_K, LANES), i32),
                   jax.ShapeDtypeStruct((T // LANES, TOP_K, LANES), i32),
                   jax.ShapeDtypeStruct((T, TOP_K), f32),
                   jax.ShapeDtypeStruct((E, LANES), f32)],
        scratch_shapes=[pltpu.VMEM((E, LANES), f32)],
        compiler_params=pltpu.CompilerParams(dimension_semantics=("arbitrary",), vmem_limit_bytes=VMEM_LIMIT),
        name="route",
    )(x1.reshape(T, D), mod_f, g_pre.reshape(1, D), w_router.T.astype(bf16), router_bias.reshape(E, 1))


def _plan_kernel(cnt_ref, eidx_ref, rank_ref, dest_ref, tstart_ref, total_ref, pstart_ref, *, E, TM, nb):
    i = pl.program_id(0)

    @pl.when(i == 0)
    def _():
        cnt_col = cnt_ref[:, 0:1]
        pad_col = jnp.ceil(cnt_col / TM) * TM
        r = lax.broadcasted_iota(i32, (E, E), 0)
        c = lax.broadcasted_iota(i32, (E, E), 1)
        pad_row = jnp.sum(jnp.where(r == c, pad_col, 0.0), axis=0, keepdims=True)
        pstart = jnp.sum(jnp.where(c < r, pad_row, 0.0), axis=1, keepdims=True)
        pstart_ref[...] = jnp.broadcast_to(pstart, (E, LANES))
        pstart_row = jnp.sum(jnp.where(r == c, pstart, 0.0), axis=0, keepdims=True)
        tstart_ref[...] = (pstart_row / TM).astype(i32)
        total = jnp.sum(pad_col, axis=0, keepdims=True)
        total_ref[...] = jnp.broadcast_to(total / TM, (1, LANES)).astype(i32)

    ei = lax.broadcasted_iota(i32, (E, LANES), 0)
    pstart = pstart_ref[...]
    for j in range(nb):
        for k in range(TOP_K):
            e_row = eidx_ref[j, k:k + 1, :]
            base = jnp.sum(jnp.where(ei == e_row, pstart, 0.0), axis=0, keepdims=True).astype(i32)
            dest_ref[j, k:k + 1, :] = base + rank_ref[j, k:k + 1, :]


def _plan(cnt, eidx, rank, *, TM):
    E = cnt.shape[0]
    NB = eidx.shape[0]
    nb = min(16, NB)
    assert NB % nb == 0
    kern = functools.partial(_plan_kernel, E=E, TM=TM, nb=nb)
    blk = pl.BlockSpec((nb, TOP_K, LANES), lambda i: (i, 0, 0))
    dest, tstart, total = pl.pallas_call(
        kern,
        grid=(NB // nb,),
        in_specs=[pl.BlockSpec((E, LANES), lambda i: (0, 0)), blk, blk],
        out_specs=[blk,
                   pl.BlockSpec((1, E), lambda i: (0, 0)),
                   pl.BlockSpec((1, LANES), lambda i: (0, 0))],
        out_shape=[jax.ShapeDtypeStruct((NB, TOP_K, LANES), i32),
                   jax.ShapeDtypeStruct((1, E), i32),
                   jax.ShapeDtypeStruct((1, LANES), i32)],
        scratch_shapes=[pltpu.VMEM((E, LANES), f32)],
        compiler_params=pltpu.CompilerParams(dimension_semantics=("arbitrary",)),
        name="plan",
    )(cnt, eidx, rank)
    return dest, jnp.concatenate([tstart.reshape(E), total.reshape(LANES)[0:1]])


def _sc_workers():
    info = plsc.get_sparse_core_info()
    return info.num_cores, info.num_subcores


def _dispatch(h2p, dest, NP):
    T, W = h2p.shape
    NC, NS = _sc_workers()
    nblk = T // (NC * NS * SC_ROWS)
    assert T == nblk * NC * NS * SC_ROWS
    mesh = plsc.VectorSubcoreMesh(core_axis_name="c", subcore_axis_name="s")

    @functools.partial(
        pl.kernel, mesh=mesh, out_type=jax.ShapeDtypeStruct((NP, W), h2p.dtype),
        scratch_types=[pltpu.VMEM((TOP_K, SC_ROWS), i32), pltpu.VMEM((SC_ROWS, W), h2p.dtype)],
        name="dispatch")
    def kern(h_hbm, dest_hbm, xs_hbm, idx_v, rows_v):
        wid = lax.axis_index("s") * NC + lax.axis_index("c")

        @pl.loop(0, nblk)
        def _(b):
            blk = wid * nblk + b
            pltpu.sync_copy(dest_hbm.at[blk], idx_v)
            pltpu.sync_copy(h_hbm.at[pl.ds(pl.multiple_of(blk * SC_ROWS, SC_ROWS), SC_ROWS)], rows_v)
            for k in range(TOP_K):
                pltpu.sync_copy(rows_v, xs_hbm.at[idx_v.at[k]])

    return kern(h2p, dest)


def _collect(ys, dest, t0, tc):
    NP, W = ys.shape
    NC, NS = _sc_workers()
    nblk = tc // (NC * NS * SC_ROWS)
    assert tc == nblk * NC * NS * SC_ROWS and t0 % SC_ROWS == 0
    blk0 = t0 // SC_ROWS
    mesh = plsc.VectorSubcoreMesh(core_axis_name="c", subcore_axis_name="s")

    @functools.partial(
        pl.kernel, mesh=mesh, out_type=jax.ShapeDtypeStruct((TOP_K, tc, W), ys.dtype),
        scratch_types=[pltpu.VMEM((TOP_K, SC_ROWS), i32), pltpu.VMEM((SC_ROWS, W), ys.dtype)],
        name="collect")
    def kern(ys_hbm, dest_hbm, yg_hbm, idx_v, rows_v):
        wid = lax.axis_index("s") * NC + lax.axis_index("c")

        @pl.loop(0, nblk)
        def _(b):
            blk = wid * nblk + b
            pltpu.sync_copy(dest_hbm.at[blk0 + blk], idx_v)
            for k in range(TOP_K):
                pltpu.sync_copy(ys_hbm.at[idx_v.at[k]], rows_v)
                pltpu.sync_copy(rows_v, yg_hbm.at[k, pl.ds(pl.multiple_of(blk * SC_ROWS, SC_ROWS), SC_ROWS)])

    return kern(ys, dest)


def _experts_kernel(tstart_ref, xs_hbm, wgu_ref, wd_ref, ys_hbm, xbuf, ybuf, actbuf, state, sem_in, sem_out,
                    wgu_bf, wd_bf, *, TM, ED, D):
    e = pl.program_id(0)
    n_exp = pl.num_programs(0)
    u_lo, u_hi, n_tiles = tstart_ref[e], tstart_ref[e + 1], tstart_ref[n_exp]
    sub = TM // N_SUB

    def in_copy(u, slot):
        rows = pl.ds(pl.multiple_of(u * TM, TM), TM)
        return pltpu.make_async_copy(xs_hbm.at[rows], xbuf.at[slot], sem_in.at[slot])

    def out_copy(u, slot):
        rows = pl.ds(pl.multiple_of(u * TM, TM), TM)
        return pltpu.make_async_copy(ybuf.at[slot], ys_hbm.at[rows], sem_out.at[slot])

    def wait_in(u):
        in_copy(u, lax.rem(u, N_IN)).wait()

    def start_in_ahead(u):
        ahead = u + (N_IN - 1)

        @pl.when(ahead < n_tiles)
        def _():
            in_copy(ahead, lax.rem(ahead, N_IN)).start()

    def wait_out_slot(u):
        @pl.when(u >= N_OUT)
        def _():
            out_copy(u - N_OUT, lax.rem(u, N_OUT)).wait()

    def start_out(u):
        out_copy(u, lax.rem(u, N_OUT)).start()

    def up(u):
        slot, aslot = lax.rem(u, N_IN), lax.rem(u, 2)
        for j in range(N_SUB):
            rows = slice(j * sub, (j + 1) * sub)
            lo, hi = _unpack_bf16_pair(xbuf[slot, rows, :])
            gu = (jnp.dot(lo.astype(bf16), wgu_bf[0:D // 2, :], preferred_element_type=f32)
                  + jnp.dot(hi.astype(bf16), wgu_bf[D // 2:, :], preferred_element_type=f32))
            actbuf[aslot, rows, :] = (_silu(gu[:, :ED]) * gu[:, ED:]).astype(bf16)

    def down(u, wslot):
        oslot, aslot = lax.rem(u, N_OUT), lax.rem(u, 2)
        for j in range(N_SUB):
            rows = slice(j * sub, (j + 1) * sub)
            y = jnp.dot(actbuf[aslot, rows, :], wd_bf[wslot], preferred_element_type=f32)
            ybuf[oslot, rows, :] = _pack_bf16_pair(y[:, :D // 2], y[:, D // 2:])

    @pl.when(e == 0)
    def _():
        state[0] = 0
        state[1] = 0
        for u in range(N_IN - 1):
            @pl.when(u < n_tiles)
            def _():
                in_copy(u, u).start()

    @pl.when(u_hi > u_lo)
    def _():
        old_w = state[1]
        new_w = 1 - old_w

        @pl.when(state[0] == 1)
        def _():
            wait_in(u_lo)
            wait_out_slot(u_lo - 1)
            down(u_lo - 1, old_w)
            wgu_bf[...] = wgu_ref[0].astype(bf16)
            wd_bf[new_w] = wd_ref[0].astype(bf16)
            up(u_lo)
            start_out(u_lo - 1)
            start_in_ahead(u_lo)

        @pl.when(state[0] == 0)
        def _():
            wait_in(u_lo)
            wgu_bf[...] = wgu_ref[0].astype(bf16)
            wd_bf[new_w] = wd_ref[0].astype(bf16)
            up(u_lo)
            start_in_ahead(u_lo)

        def skewed(u, carry):
            wait_in(u)
            wait_out_slot(u - 1)
            down(u - 1, new_w)
            up(u)
            start_out(u - 1)
            start_in_ahead(u)
            return carry

        n_rest = u_hi - (u_lo + 1)

        def skewed_pair(i, carry):
            u = u_lo + 1 + 2 * i
            wait_in(u)
            wait_in(u + 1)
            wait_out_slot(u - 1)
            wait_out_slot(u)
            down(u - 1, new_w)
            up(u)
            down(u, new_w)
            up(u + 1)
            start_out(u - 1)
            start_out(u)
            start_in_ahead(u)
            start_in_ahead(u + 1)
            return carry

        lax.fori_loop(0, lax.shift_right_logical(n_rest, 1), skewed_pair, 0)

        @pl.when(lax.rem(n_rest, 2) == 1)
        def _():
            skewed(u_hi - 1, 0)

        state[0] = 1
        state[1] = new_w

    @pl.when(e == n_exp - 1)
    def _():
        @pl.when(state[0] == 1)
        def _():
            wait_out_slot(n_tiles - 1)
            down(n_tiles - 1, state[1])
            start_out(n_tiles - 1)

        for back in range(N_OUT, 0, -1):
            @pl.when(n_tiles >= back)
            def _():
                out_copy(n_tiles - back, lax.rem(n_tiles - back, N_OUT)).wait()


def _experts(xs, tstart, w_gate_up, w_down, *, TM):
    NP, W = xs.shape
    E, D, ED2 = w_gate_up.shape
    ED = ED2 // 2
    kern = functools.partial(_experts_kernel, TM=TM, ED=ED, D=D)
    return pl.pallas_call(
        kern,
        grid_spec=pltpu.PrefetchScalarGridSpec(
            num_scalar_prefetch=1,
            grid=(E,),
            in_specs=[pl.BlockSpec(memory_space=pl.ANY),
                      pl.BlockSpec((1, D, ED2), lambda e, ts: (e, 0, 0)),
                      pl.BlockSpec((1, ED, D), lambda e, ts: (e, 0, 0))],
            out_specs=pl.BlockSpec(memory_space=pl.ANY),
            scratch_shapes=[pltpu.VMEM((N_IN, TM, W), i32), pltpu.VMEM((N_OUT, TM, W), i32),
                            pltpu.VMEM((2, TM, ED), bf16), pltpu.SMEM((2,), i32),
                            pltpu.SemaphoreType.DMA((N_IN,)), pltpu.SemaphoreType.DMA((N_OUT,)),
                            pltpu.VMEM((D, ED2), bf16), pltpu.VMEM((2, ED, D), bf16)]),
        out_shape=jax.ShapeDtypeStruct((NP, W), i32),
        compiler_params=pltpu.CompilerParams(dimension_semantics=("arbitrary",), vmem_limit_bytes=VMEM_LIMIT),
        name="experts",
    )(tstart, xs, w_gate_up, w_down)


def _final_kernel(yg_ref, wcol_ref, h2p_ref, x1_ref, mod_ref, gpost_ref, wsgu_ref, wsd_ref, o_ref, *, D, SD):
    wcol = wcol_ref[...]
    r_lo = jnp.zeros(h2p_ref.shape, f32)
    r_hi = jnp.zeros(h2p_ref.shape, f32)
    for k in range(TOP_K):
        lo, hi = _unpack_bf16_pair(yg_ref[k])
        wk = wcol[:, k:k + 1]
        r_lo = r_lo + wk * lo
        r_hi = r_hi + wk * hi
    routed = jnp.concatenate([r_lo, r_hi], axis=-1)

    hlo, hhi = _unpack_bf16_pair(h2p_ref[...])
    gs = (jnp.dot(hlo.astype(bf16), wsgu_ref[0:D // 2, :], preferred_element_type=f32)
          + jnp.dot(hhi.astype(bf16), wsgu_ref[D // 2:, :], preferred_element_type=f32))
    sh = (_silu(gs[:, :SD]) * gs[:, SD:]).astype(bf16)
    y = routed + jnp.dot(sh, wsd_ref[...], preferred_element_type=f32)
    yn = y * lax.rsqrt(jnp.mean(y * y, axis=-1, keepdims=True) + EPS) * gpost_ref[...]
    o_ref[...] = x1_ref[...] + mod_ref[0, 2:3, :] * yn


def _final_kernel_chained(yg_ref, wcol_ref, h2p_ref, x1_ref, mod_ref, gpost_ref, wsgu_ref, wsd_ref, prev_ref, o_ref,
                          *, D, SD):
    del prev_ref
    _final_kernel(yg_ref, wcol_ref, h2p_ref, x1_ref, mod_ref, gpost_ref, wsgu_ref, wsd_ref, o_ref, D=D, SD=SD)


def _final(yg, wcol, h2p, x1, mod_f, g_post, wsgu, wsd, out_prev, *, t0, tt, S):
    T, D = x1.shape
    tc = yg.shape[1]
    SD = wsd.shape[0]
    spt = S // tt
    i0 = t0 // tt
    assert t0 % tt == 0 and tc % tt == 0
    kern = functools.partial(_final_kernel, D=D, SD=SD)
    in_specs = [pl.BlockSpec((TOP_K, tt, D // 2), lambda i: (0, i, 0)),
                pl.BlockSpec((tt, TOP_K), lambda i: (i0 + i, 0)),
                pl.BlockSpec((tt, D // 2), lambda i: (i0 + i, 0)),
                pl.BlockSpec((tt, D), lambda i: (i0 + i, 0)),
                pl.BlockSpec((1, 3, D), lambda i: ((i0 + i) // spt, 0, 0)),
                pl.BlockSpec((1, D), lambda i: (0, 0)),
                pl.BlockSpec((D, 2 * SD), lambda i: (0, 0)),
                pl.BlockSpec((SD, D), lambda i: (0, 0))]
    args = [yg, wcol, h2p, x1, mod_f, g_post.reshape(1, D), wsgu, wsd]
    aliases = {}
    if out_prev is not None:
        in_specs.append(pl.BlockSpec(memory_space=pl.ANY))
        args.append(out_prev)
        aliases = {len(args) - 1: 0}
        kern = functools.partial(_final_kernel_chained, D=D, SD=SD)
    return pl.pallas_call(
        kern,
        grid=(tc // tt,),
        in_specs=in_specs,
        out_specs=pl.BlockSpec((tt, D), lambda i: (i0 + i, 0)),
        out_shape=jax.ShapeDtypeStruct((T, D), f32),
        input_output_aliases=aliases,
        compiler_params=pltpu.CompilerParams(dimension_semantics=("arbitrary",), vmem_limit_bytes=VMEM_LIMIT),
        name="final",
    )(*args)


def _layer(x, mod, g_pre_mix, g_post_mix, g_pre_ffn, g_post_ffn, w_in, w_dw, b_dw, ln_g, ln_b, w_conv_out,
           w_pool, b_pool, pool_scale, w_out, w_router, router_bias, w_gate_up, w_down, ws_gate_up, ws_down):
    B, S, D = x.shape
    T = B * S
    E = w_router.shape[1]
    mod6 = mod.reshape(B, 6, D)
    mod_m, mod_f = mod6[:, 0:3], mod6[:, 3:6]
    x1 = _mixer(x, mod_m, g_pre_mix, g_post_mix, w_in, w_dw, b_dw, ln_g, ln_b, w_conv_out, w_pool, b_pool,
                pool_scale, w_out, ts=min(512, S))
    h2p, eidx, rank, wcol, cnt = _route(x1, mod_f[:, 0:2], g_pre_ffn, w_router, router_bias, tt=min(512, S))
    TM = 512
    NT = (T * TOP_K) // TM + E
    dest, tstart = _plan(cnt, eidx, rank, TM=TM)
    xs = _dispatch(h2p, dest, NT * TM)
    ys = _experts(xs, tstart, w_gate_up, w_down, TM=TM)
    x1f = x1.reshape(T, D)
    wsgu, wsd = ws_gate_up.astype(bf16), ws_down.astype(bf16)
    tt = min(256, S)
    n_chunks = N_TAIL_CHUNKS
    while (T // n_chunks) % tt:
        n_chunks //= 2
    tc = T // n_chunks
    out = None
    for ci in range(n_chunks):
        yg = _collect(ys, dest, ci * tc, tc)
        out = _final(yg, wcol, h2p, x1f, mod_f, g_post_ffn, wsgu, wsd, out, t0=ci * tc, tt=tt, S=S)
    return out.reshape(B, S, D)


def kernel(x, c, w_ada, b_ada, g_pre_mix, g_post_mix, g_pre_ffn, g_post_ffn, w_in, w_dw, b_dw, ln_g, ln_b,
           w_conv_out, w_pool, b_pool, pool_scale, w_out, w_router, router_bias, w_gate_up, w_down, ws_gate_up,
           ws_down):
    depth = w_ada.shape[0]
    for l in range(depth):
        mod = _adaln(c, w_ada[l], b_ada[l])
        x = _layer(x, mod, g_pre_mix[l], g_post_mix[l], g_pre_ffn[l], g_post_ffn[l], w_in[l], w_dw[l], b_dw[l],
                   ln_g[l], ln_b[l], w_conv_out[l], w_pool[l], b_pool[l], pool_scale[l], w_out[l], w_router[l],
                   router_bias[l], w_gate_up[l], w_down[l], ws_gate_up[l], ws_down[l])
    return x
```

```python
import functools

import jax
import jax.numpy as jnp
from jax import lax
from jax.experimental import pallas as pl
from jax.experimental.pallas import tpu as pltpu
from jax.experimental.pallas import tpu_sc as plsc

EPS = 1e-6
TOP_K = 8
N_EXPERT_GROUPS = 8
TOPK_GROUPS = 4
ROUTED_SCALE = 2.5
POOL_WINDOWS = (2, 4, 8, 16)

LANES = 128
SUBLANES = 8
CONV_HIST = 32
POOL_HIST = 16
CONV_ROWS = 32
N_IN = 6
N_OUT = 2
N_SUB = 2
N_TAIL_CHUNKS = 8
SC_ROWS = 128
VMEM_LIMIT = 56 * 1024 * 1024

f32 = jnp.float32
bf16 = jnp.bfloat16
i32 = jnp.int32


def _sigmoid(v):
    return 0.5 * jnp.tanh(0.5 * v) + 0.5


def _silu(v):
    return v * _sigmoid(v)


def _pack_bf16_pair(lo, hi):
    lo_bits = lax.bitcast_convert_type(lo.astype(bf16).astype(f32), jnp.uint32)
    hi_bits = lax.bitcast_convert_type(hi.astype(bf16).astype(f32), jnp.uint32)
    word = lax.shift_right_logical(lo_bits, jnp.uint32(16)) | hi_bits
    return lax.bitcast_convert_type(word, i32)


def _unpack_bf16_pair(word):
    w = lax.bitcast_convert_type(word, jnp.uint32)
    lo = lax.bitcast_convert_type(lax.shift_left(w, jnp.uint32(16)), f32)
    hi = lax.bitcast_convert_type(w & jnp.uint32(0xFFFF0000), f32)
    return lo, hi


def _adaln_kernel(c_ref, w_ref, b_ref, o_ref):
    cond = _silu(c_ref[...])
    o_ref[...] = jnp.dot(cond.astype(bf16), w_ref[...].astype(bf16), preferred_element_type=f32) + b_ref[...]


def _adaln(c, w_ada, b_ada):
    B, D = c.shape
    N = w_ada.shape[1]
    tn = 1024
    return pl.pallas_call(
        _adaln_kernel,
        grid=(N // tn,),
        in_specs=[pl.BlockSpec((B, D), lambda j: (0, 0)),
                  pl.BlockSpec((D, tn), lambda j: (0, j)),
                  pl.BlockSpec((1, tn), lambda j: (0, j))],
        out_specs=pl.BlockSpec((B, tn), lambda j: (0, j)),
        out_shape=jax.ShapeDtypeStruct((B, N), f32),
        name="adaln",
    )(c, w_ada, b_ada.reshape(1, N))


def _mixer_kernel(x_ref, mod_ref, gpre_ref, gpost_ref, win_ref, wdw_ref, bdw_ref, lng_ref, lnb_ref,
                  wco_ref, wpool_ref, bpool_ref, pscale_ref, wout_ref, o_ref,
                  aext_ref, ashift_ref, pext_ref, act_ref, *, ts, D, conv_k):
    s = pl.program_id(1)

    @pl.when(s == 0)
    def _():
        aext_ref[0:CONV_HIST, :] = jnp.zeros((CONV_HIST, D), f32)
        pext_ref[0:POOL_HIST, :] = jnp.zeros((POOL_HIST, D), f32)

    x = x_ref[0]
    shift, scale, gate = mod_ref[0, 0:1, :], mod_ref[0, 1:2, :], mod_ref[0, 2:3, :]
    hb = (x * lax.rsqrt(jnp.mean(x * x, axis=-1, keepdims=True) + EPS) * (gpre_ref[...] * (1.0 + scale))
          + shift).astype(bf16)

    a_val = jnp.dot(hb, win_ref[:, 0:D], preferred_element_type=f32)
    a_gate = jnp.dot(hb, win_ref[:, D:2 * D], preferred_element_type=f32)
    aext_ref[CONV_HIST:CONV_HIST + ts, :] = a_val * _sigmoid(a_gate)
    n_ext = ts + CONV_HIST
    a_full = aext_ref[...]
    for r in range(1, SUBLANES):
        ashift_ref[r - 1] = pltpu.roll(a_full, n_ext - r, axis=0)
    rc = CONV_ROWS
    off0 = CONV_HIST - (conv_k - 1)
    for c in range(ts // rc):
        acc = jnp.broadcast_to(bdw_ref[...], (rc, D))
        for k in range(conv_k):
            r, q = (off0 + k) % SUBLANES, (off0 + k) // SUBLANES
            row0 = c * rc + q * SUBLANES
            tap = aext_ref[row0:row0 + rc, :] if r == 0 else ashift_ref[r - 1, row0:row0 + rc, :]
            acc = acc + jnp.concatenate([wdw_ref[k]] * (rc // SUBLANES), axis=0) * tap
        mu = jnp.mean(acc, axis=-1, keepdims=True)
        cen = acc - mu
        var = jnp.mean(cen * cen, axis=-1, keepdims=True)
        ln = cen * lax.rsqrt(var + EPS) * lng_ref[...] + lnb_ref[...]
        act_ref[c * rc:(c + 1) * rc, :] = _silu(ln).astype(bf16)
    aext_ref[0:CONV_HIST, :] = aext_ref[ts:ts + CONV_HIST, :]
    a_out = jnp.dot(act_ref[...], wco_ref[...], preferred_element_type=f32)

    pext_ref[POOL_HIST:POOL_HIST + ts, :] = jnp.dot(hb, win_ref[:, 2 * D:3 * D], preferred_element_type=f32)
    t_glob = s * ts + lax.broadcasted_iota(i32, (ts, 1), 0)
    dg = D // len(POOL_WINDOWS)
    bms = []
    for g, win in enumerate(POOL_WINDOWS):
        cols = slice(g * dg, (g + 1) * dg)
        p_ext = pext_ref[:, cols]
        wsum, j = p_ext, 1
        while j < win:
            wsum = wsum + pltpu.roll(wsum, j, axis=0)
            j *= 2
        cur = p_ext[POOL_HIST:, :]
        cnt = jnp.minimum(t_glob + 1, win).astype(f32)
        pooled = wsum[POOL_HIST:, :] / cnt - cur
        bms.append(jnp.dot(pooled.astype(bf16), wpool_ref[g], preferred_element_type=f32))
    pext_ref[0:POOL_HIST, :] = pext_ref[ts:ts + POOL_HIST, :]
    bm = (jnp.concatenate(bms, axis=-1) + bpool_ref[...]) * pscale_ref[...]

    g_a = _sigmoid(jnp.dot(hb, win_ref[:, 3 * D:4 * D], preferred_element_type=f32))
    g_b = _sigmoid(jnp.dot(hb, win_ref[:, 4 * D:5 * D], preferred_element_type=f32))
    mixed = (g_a * a_out + g_b * bm).astype(bf16)
    y = jnp.dot(mixed, wout_ref[...], preferred_element_type=f32)
    o_ref[0] = x + y * lax.rsqrt(jnp.mean(y * y, axis=-1, keepdims=True) + EPS) * (gpost_ref[...] * gate)


def _mixer(x, mod_m, g_pre, g_post, w_in, w_dw, b_dw, ln_g, ln_b, w_conv_out, w_pool, b_pool, pool_scale,
           w_out, *, ts):
    B, S, D = x.shape
    conv_k = w_dw.shape[0]
    assert conv_k - 1 <= CONV_HIST and max(POOL_WINDOWS) - 1 <= POOL_HIST
    assert S % ts == 0 and ts >= CONV_HIST and D % (LANES * len(POOL_WINDOWS)) == 0
    row = lambda v: v.reshape(1, D)
    vm = pl.BlockSpec(memory_space=pltpu.VMEM)
    kern = functools.partial(_mixer_kernel, ts=ts, D=D, conv_k=conv_k)
    return pl.pallas_call(
        kern,
        grid=(B, S // ts),
        in_specs=[pl.BlockSpec((1, ts, D), lambda b, s: (b, s, 0)),
                  pl.BlockSpec((1, 3, D), lambda b, s: (b, 0, 0)),
                  vm, vm, vm, vm, vm, vm, vm, vm, vm, vm, vm, vm],
        out_specs=pl.BlockSpec((1, ts, D), lambda b, s: (b, s, 0)),
        out_shape=jax.ShapeDtypeStruct((B, S, D), f32),
        scratch_shapes=[pltpu.VMEM((ts + CONV_HIST, D), f32),
                        pltpu.VMEM((SUBLANES - 1, ts + CONV_HIST, D), f32),
                        pltpu.VMEM((ts + POOL_HIST, D), f32),
                        pltpu.VMEM((ts, D), bf16)],
        compiler_params=pltpu.CompilerParams(dimension_semantics=("arbitrary", "arbitrary"),
                                             vmem_limit_bytes=VMEM_LIMIT),
        name="mixer",
    )(x, mod_m, row(g_pre), row(g_post), w_in.astype(bf16),
      jnp.broadcast_to(w_dw[:, None, :], (conv_k, SUBLANES, D)), row(b_dw), row(ln_g), row(ln_b),
      w_conv_out.astype(bf16), w_pool.astype(bf16), row(b_pool), row(pool_scale), w_out.astype(bf16))


def _route_kernel(x_ref, mod_ref, gpre_ref, wr_ref, bias_ref,
                  h2p_ref, eidx_ref, rank_ref, wcol_ref, cnt_ref, carry_ref, *, tt, D, E):
    i = pl.program_id(0)

    @pl.when(i == 0)
    def _():
        carry_ref[...] = jnp.zeros((E, LANES), f32)

    x = x_ref[...]
    shift, scale = mod_ref[0, 0:1, :], mod_ref[0, 1:2, :]
    h = x * lax.rsqrt(jnp.mean(x * x, axis=-1, keepdims=True) + EPS) * gpre_ref[...]
    h = h * (1.0 + scale) + shift
    h2p_ref[...] = _pack_bf16_pair(h[:, :D // 2], h[:, D // 2:])

    logits = lax.dot_general(wr_ref[...], h.astype(bf16), (((1,), (1,)), ((), ())),
                             preferred_element_type=f32)
    scores = _sigmoid(logits)
    choice = scores + bias_ref[...]
    G = N_EXPERT_GROUPS
    ge = E // G
    neg = jnp.float32(-jnp.inf)

    ch3 = choice.reshape(G, ge, tt)
    m1 = jnp.max(ch3, axis=1, keepdims=True)
    is_m1 = ch3 == m1
    n_m1 = jnp.sum(is_m1.astype(f32), axis=1, keepdims=True)
    m2 = jnp.max(jnp.where(is_m1, neg, ch3), axis=1, keepdims=True)
    gs = (m1 + jnp.where(n_m1 >= 2.0, m1, m2)).reshape(G, tt)

    gi = lax.broadcasted_iota(i32, (G, tt), 0).astype(f32)
    gsel = jnp.zeros((G, tt), f32)
    for _ in range(TOPK_GROUPS):
        gm = jnp.max(gs, axis=0, keepdims=True)
        first = jnp.min(jnp.where(gs == gm, gi, float(G)), axis=0, keepdims=True)
        hit = gi == first
        gsel = jnp.where(hit, 1.0, gsel)
        gs = jnp.where(hit, neg, gs)

    emask = jnp.broadcast_to(gsel.reshape(G, 1, tt), (G, ge, tt)).reshape(E, tt)
    masked = jnp.where(emask > 0.5, choice, neg)
    ei = lax.broadcasted_iota(i32, (E, tt), 0).astype(f32)
    sel = jnp.zeros((E, tt), f32)
    idx_rows, sc_rows = [], []
    for _ in range(TOP_K):
        mx = jnp.max(masked, axis=0, keepdims=True)
        first = jnp.min(jnp.where(masked == mx, ei, float(E)), axis=0, keepdims=True)
        hit = ei == first
        idx_rows.append(first)
        sc_rows.append(jnp.sum(jnp.where(hit, scores, 0.0), axis=0, keepdims=True))
        sel = sel + hit.astype(f32)
        masked = jnp.where(hit, neg, masked)

    upper = (lax.broadcasted_iota(i32, (tt, tt), 0) < lax.broadcasted_iota(i32, (tt, tt), 1)).astype(bf16)
    selb = sel.astype(bf16)
    excl = jnp.dot(selb, upper, preferred_element_type=f32)
    carry = carry_ref[...]
    pos = excl + jnp.concatenate([carry] * (tt // LANES), axis=1)
    carry = carry + jnp.dot(selb, jnp.ones((tt, LANES), bf16), preferred_element_type=f32)
    carry_ref[...] = carry
    cnt_ref[...] = carry

    ssum = sc_rows[0]
    for r in sc_rows[1:]:
        ssum = ssum + r
    wscale = ROUTED_SCALE / ssum
    nb = tt // LANES
    for k in range(TOP_K):
        rk = jnp.sum(jnp.where(ei == idx_rows[k], pos, 0.0), axis=0, keepdims=True).astype(i32)
        for j in range(nb):
            eidx_ref[j, k:k + 1, :] = idx_rows[k][:, j * LANES:(j + 1) * LANES].astype(i32)
            rank_ref[j, k:k + 1, :] = rk[:, j * LANES:(j + 1) * LANES]
    wstack = jnp.concatenate([r * wscale for r in sc_rows] + [jnp.zeros((LANES - TOP_K, tt), f32)], axis=0)
    for j in range(nb):
        blk = wstack[:, j * LANES:(j + 1) * LANES].T
        wcol_ref[j * LANES:(j + 1) * LANES, :] = blk[:, 0:TOP_K]


def _route(x1, mod_f, g_pre, w_router, router_bias, *, tt):
    B, S, D = x1.shape
    T = B * S
    E = w_router.shape[1]
    assert S % tt == 0 and tt % LANES == 0 and E % (N_EXPERT_GROUPS * SUBLANES) == 0
    spt = S // tt
    nb = tt // LANES
    kern = functools.partial(_route_kernel, tt=tt, D=D, E=E)
    return pl.pallas_call(
        kern,
        grid=(T // tt,),
        in_specs=[pl.BlockSpec((tt, D), lambda i: (i, 0)),
                  pl.BlockSpec((1, 2, D), lambda i: (i // spt, 0, 0)),
                  pl.BlockSpec((1, D), lambda i: (0, 0)),
                  pl.BlockSpec((E, D), lambda i: (0, 0)),
                  pl.BlockSpec((E, 1), lambda i: (0, 0))],
        out_specs=[pl.BlockSpec((tt, D // 2), lambda i: (i, 0)),
                   pl.BlockSpec((nb, TOP_K, LANES), lambda i: (i, 0, 0)),
                   pl.BlockSpec((nb, TOP_K, LANES), lambda i: (i, 0, 0)),
                   pl.BlockSpec((tt, TOP_K), lambda i: (i, 0)),
                   pl.BlockSpec((E, LANES), lambda i: (0, 0))],
        out_shape=[jax.ShapeDtypeStruct((T, D // 2), i32),
                   jax.ShapeDtypeStruct((T // LANES, TOP_K, LANES), i32),
                   jax.ShapeDtypeStruct((T // LANES, TOP_K, LANES), i32),
                   jax.ShapeDtypeStruct((T, TOP_K), f32),
                   jax.ShapeDtypeStruct((E, LANES), f32)],
        scratch_shapes=[pltpu.VMEM((E, LANES), f32)],
        compiler_params=pltpu.CompilerParams(dimension_semantics=("arbitrary",), vmem_limit_bytes=VMEM_LIMIT),
        name="route",
    )(x1.reshape(T, D), mod_f, g_pre.reshape(1, D), w_router.T.astype(bf16), router_bias.reshape(E, 1))


def _plan_kernel(cnt_ref, eidx_ref, rank_ref, dest_ref, tstart_ref, total_ref, pstart_ref, *, E, TM, nb):
    i = pl.program_id(0)

    @pl.when(i == 0)
    def _():
        cnt_col = cnt_ref[:, 0:1]
        pad_col = jnp.ceil(cnt_col / TM) * TM
        r = lax.broadcasted_iota(i32, (E, E), 0)
        c = lax.broadcasted_iota(i32, (E, E), 1)
        pad_row = jnp.sum(jnp.where(r == c, pad_col, 0.0), axis=0, keepdims=True)
        pstart = jnp.sum(jnp.where(c < r, pad_row, 0.0), axis=1, keepdims=True)
        pstart_ref[...] = jnp.broadcast_to(pstart, (E, LANES))
        pstart_row = jnp.sum(jnp.where(r == c, pstart, 0.0), axis=0, keepdims=True)
        tstart_ref[...] = (pstart_row / TM).astype(i32)
        total = jnp.sum(pad_col, axis=0, keepdims=True)
        total_ref[...] = jnp.broadcast_to(total / TM, (1, LANES)).astype(i32)

    ei = lax.broadcasted_iota(i32, (E, LANES), 0)
    pstart = pstart_ref[...]
    for j in range(nb):
        for k in range(TOP_K):
            e_row = eidx_ref[j, k:k + 1, :]
            base = jnp.sum(jnp.where(ei == e_row, pstart, 0.0), axis=0, keepdims=True).astype(i32)
            dest_ref[j, k:k + 1, :] = base + rank_ref[j, k:k + 1, :]


def _plan(cnt, eidx, rank, *, TM):
    E = cnt.shape[0]
    NB = eidx.shape[0]
    nb = min(16, NB)
    assert NB % nb == 0
    kern = functools.partial(_plan_kernel, E=E, TM=TM, nb=nb)
    blk = pl.BlockSpec((nb, TOP_K, LANES), lambda i: (i, 0, 0))
    dest, tstart, total = pl.pallas_call(
        kern,
        grid=(NB // nb,),
        in_specs=[pl.BlockSpec((E, LANES), lambda i: (0, 0)), blk, blk],
        out_specs=[blk,
                   pl.BlockSpec((1, E), lambda i: (0, 0)),
                   pl.BlockSpec((1, LANES), lambda i: (0, 0))],
        out_shape=[jax.ShapeDtypeStruct((NB, TOP_K, LANES), i32),
                   jax.ShapeDtypeStruct((1, E), i32),
                   jax.ShapeDtypeStruct((1, LANES), i32)],
        scratch_shapes=[pltpu.VMEM((E, LANES), f32)],
        compiler_params=pltpu.CompilerParams(dimension_semantics=("arbitrary",)),
        name="plan",
    )(cnt, eidx, rank)
    return dest, jnp.concatenate([tstart.reshape(E), total.reshape(LANES)[0:1]])


def _sc_workers():
    info = plsc.get_sparse_core_info()
    return info.num_cores, info.num_subcores


def _dispatch(h2p, dest, NP):
    T, W = h2p.shape
    NC, NS = _sc_workers()
    nblk = T // (NC * NS * SC_ROWS)
    assert T == nblk * NC * NS * SC_ROWS
    mesh = plsc.VectorSubcoreMesh(core_axis_name="c", subcore_axis_name="s")

    @functools.partial(
        pl.kernel, mesh=mesh, out_type=jax.ShapeDtypeStruct((NP, W), h2p.dtype),
        scratch_types=[pltpu.VMEM((TOP_K, SC_ROWS), i32), pltpu.VMEM((SC_ROWS, W), h2p.dtype)],
        name="dispatch")
    def kern(h_hbm, dest_hbm, xs_hbm, idx_v, rows_v):
        wid = lax.axis_index("s") * NC + lax.axis_index("c")

        @pl.loop(0, nblk)
        def _(b):
            blk = wid * nblk + b
            pltpu.sync_copy(dest_hbm.at[blk], idx_v)
            pltpu.sync_copy(h_hbm.at[pl.ds(pl.multiple_of(blk * SC_ROWS, SC_ROWS), SC_ROWS)], rows_v)
            for k in range(TOP_K):
                pltpu.sync_copy(rows_v, xs_hbm.at[idx_v.at[k]])

    return kern(h2p, dest)


def _collect(ys, dest, t0, tc):
    NP, W = ys.shape
    NC, NS = _sc_workers()
    nblk = tc // (NC * NS * SC_ROWS)
    assert tc == nblk * NC * NS * SC_ROWS and t0 % SC_ROWS == 0
    blk0 = t0 // SC_ROWS
    mesh = plsc.VectorSubcoreMesh(core_axis_name="c", subcore_axis_name="s")

    @functools.partial(
        pl.kernel, mesh=mesh, out_type=jax.ShapeDtypeStruct((TOP_K, tc, W), ys.dtype),
        scratch_types=[pltpu.VMEM((TOP_K, SC_ROWS), i32), pltpu.VMEM((SC_ROWS, W), ys.dtype)],
        name="collect")
    def kern(ys_hbm, dest_hbm, yg_hbm, idx_v, rows_v):
        wid = lax.axis_index("s") * NC + lax.axis_index("c")

        @pl.loop(0, nblk)
        def _(b):
            blk = wid * nblk + b
            pltpu.sync_copy(dest_hbm.at[blk0 + blk], idx_v)
            for k in range(TOP_K):
                pltpu.sync_copy(ys_hbm.at[idx_v.at[k]], rows_v)
                pltpu.sync_copy(rows_v, yg_hbm.at[k, pl.ds(pl.multiple_of(blk * SC_ROWS, SC_ROWS), SC_ROWS)])

    return kern(ys, dest)


def _experts_kernel(tstart_ref, xs_hbm, wgu_ref, wd_ref, ys_hbm, xbuf, ybuf, actbuf, state, sem_in, sem_out,
                    wgu_bf, wd_bf, *, TM, ED, D):
    e = pl.program_id(0)
    n_exp = pl.num_programs(0)
    u_lo, u_hi, n_tiles = tstart_ref[e], tstart_ref[e + 1], tstart_ref[n_exp]
    sub = TM // N_SUB

    def in_copy(u, slot):
        rows = pl.ds(pl.multiple_of(u * TM, TM), TM)
        return pltpu.make_async_copy(xs_hbm.at[rows], xbuf.at[slot], sem_in.at[slot])

    def out_copy(u, slot):
        rows = pl.ds(pl.multiple_of(u * TM, TM), TM)
        return pltpu.make_async_copy(ybuf.at[slot], ys_hbm.at[rows], sem_out.at[slot])

    def wait_in(u):
        in_copy(u, lax.rem(u, N_IN)).wait()

    def start_in_ahead(u):
        ahead = u + (N_IN - 1)

        @pl.when(ahead < n_tiles)
        def _():
            in_copy(ahead, lax.rem(ahead, N_IN)).start()

    def wait_out_slot(u):
        @pl.when(u >= N_OUT)
        def _():
            out_copy(u - N_OUT, lax.rem(u, N_OUT)).wait()

    def start_out(u):
        out_copy(u, lax.rem(u, N_OUT)).start()

    def up(u):
        slot, aslot = lax.rem(u, N_IN), lax.rem(u, 2)
        for j in range(N_SUB):
            rows = slice(j * sub, (j + 1) * sub)
            lo, hi = _unpack_bf16_pair(xbuf[slot, rows, :])
            gu = (jnp.dot(lo.astype(bf16), wgu_bf[0:D // 2, :], preferred_element_type=f32)
                  + jnp.dot(hi.astype(bf16), wgu_bf[D // 2:, :], preferred_element_type=f32))
            actbuf[aslot, rows, :] = (_silu(gu[:, :ED]) * gu[:, ED:]).astype(bf16)

    def down(u, wslot):
        oslot, aslot = lax.rem(u, N_OUT), lax.rem(u, 2)
        for j in range(N_SUB):
            rows = slice(j * sub, (j + 1) * sub)
            y = jnp.dot(actbuf[aslot, rows, :], wd_bf[wslot], preferred_element_type=f32)
            ybuf[oslot, rows, :] = _pack_bf16_pair(y[:, :D // 2], y[:, D // 2:])

    @pl.when(e == 0)
    def _():
        state[0] = 0
        state[1] = 0
        for u in range(N_IN - 1):
            @pl.when(u < n_tiles)
            def _():
                in_copy(u, u).start()

    @pl.when(u_hi > u_lo)
    def _():
        old_w = state[1]
        new_w = 1 - old_w

        @pl.when(state[0] == 1)
        def _():
            wait_in(u_lo)
            wait_out_slot(u_lo - 1)
            down(u_lo - 1, old_w)
            wgu_bf[...] = wgu_ref[0].astype(bf16)
            wd_bf[new_w] = wd_ref[0].astype(bf16)
            up(u_lo)
            start_out(u_lo - 1)
            start_in_ahead(u_lo)

        @pl.when(state[0] == 0)
        def _():
            wait_in(u_lo)
            wgu_bf[...] = wgu_ref[0].astype(bf16)
            wd_bf[new_w] = wd_ref[0].astype(bf16)
            up(u_lo)
            start_in_ahead(u_lo)

        def skewed(u, carry):
            wait_in(u)
            wait_out_slot(u - 1)
            down(u - 1, new_w)
            up(u)
            start_out(u - 1)
            start_in_ahead(u)
            return carry

        n_rest = u_hi - (u_lo + 1)

        def skewed_pair(i, carry):
            u = u_lo + 1 + 2 * i
            wait_in(u)
            wait_in(u + 1)
            wait_out_slot(u - 1)
            wait_out_slot(u)
            start_in_ahead(u)
            down(u - 1, new_w)
            up(u)
            down(u, new_w)
            up(u + 1)
            start_out(u - 1)
            start_out(u)
            start_in_ahead(u + 1)
            return carry

        lax.fori_loop(0, lax.shift_right_logical(n_rest, 1), skewed_pair, 0)

        @pl.when(lax.rem(n_rest, 2) == 1)
        def _():
            skewed(u_hi - 1, 0)

        state[0] = 1
        state[1] = new_w

    @pl.when(e == n_exp - 1)
    def _():
        @pl.when(state[0] == 1)
        def _():
            wait_out_slot(n_tiles - 1)
            down(n_tiles - 1, state[1])
            start_out(n_tiles - 1)

        for back in range(N_OUT, 0, -1):
            @pl.when(n_tiles >= back)
            def _():
                out_copy(n_tiles - back, lax.rem(n_tiles - back, N_OUT)).wait()


def _experts(xs, tstart, w_gate_up, w_down, *, TM):
    NP, W = xs.shape
    E, D, ED2 = w_gate_up.shape
    ED = ED2 // 2
    kern = functools.partial(_experts_kernel, TM=TM, ED=ED, D=D)
    return pl.pallas_call(
        kern,
        grid_spec=pltpu.PrefetchScalarGridSpec(
            num_scalar_prefetch=1,
            grid=(E,),
            in_specs=[pl.BlockSpec(memory_space=pl.ANY),
                      pl.BlockSpec((1, D, ED2), lambda e, ts: (e, 0, 0)),
                      pl.BlockSpec((1, ED, D), lambda e, ts: (e, 0, 0))],
            out_specs=pl.BlockSpec(memory_space=pl.ANY),
            scratch_shapes=[pltpu.VMEM((N_IN, TM, W), i32), pltpu.VMEM((N_OUT, TM, W), i32),
                            pltpu.VMEM((2, TM, ED), bf16), pltpu.SMEM((2,), i32),
                            pltpu.SemaphoreType.DMA((N_IN,)), pltpu.SemaphoreType.DMA((N_OUT,)),
                            pltpu.VMEM((D, ED2), bf16), pltpu.VMEM((2, ED, D), bf16)]),
        out_shape=jax.ShapeDtypeStruct((NP, W), i32),
        compiler_params=pltpu.CompilerParams(dimension_semantics=("arbitrary",), vmem_limit_bytes=VMEM_LIMIT),
        name="experts",
    )(tstart, xs, w_gate_up, w_down)


def _final_kernel(yg_ref, wcol_ref, h2p_ref, x1_ref, mod_ref, gpost_ref, wsgu_ref, wsd_ref, o_ref, *, D, SD):
    wcol = wcol_ref[...]
    r_lo = jnp.zeros(h2p_ref.shape, f32)
    r_hi = jnp.zeros(h2p_ref.shape, f32)
    for k in range(TOP_K):
        lo, hi = _unpack_bf16_pair(yg_ref[k])
        wk = wcol[:, k:k + 1]
        r_lo = r_lo + wk * lo
        r_hi = r_hi + wk * hi
    routed = jnp.concatenate([r_lo, r_hi], axis=-1)

    hlo, hhi = _unpack_bf16_pair(h2p_ref[...])
    gs = (jnp.dot(hlo.astype(bf16), wsgu_ref[0:D // 2, :], preferred_element_type=f32)
          + jnp.dot(hhi.astype(bf16), wsgu_ref[D // 2:, :], preferred_element_type=f32))
    sh = (_silu(gs[:, :SD]) * gs[:, SD:]).astype(bf16)
    y = routed + jnp.dot(sh, wsd_ref[...], preferred_element_type=f32)
    yn = y * lax.rsqrt(jnp.mean(y * y, axis=-1, keepdims=True) + EPS) * gpost_ref[...]
    o_ref[...] = x1_ref[...] + mod_ref[0, 2:3, :] * yn


def _final_kernel_chained(yg_ref, wcol_ref, h2p_ref, x1_ref, mod_ref, gpost_ref, wsgu_ref, wsd_ref, prev_ref, o_ref,
                          *, D, SD):
    del prev_ref
    _final_kernel(yg_ref, wcol_ref, h2p_ref, x1_ref, mod_ref, gpost_ref, wsgu_ref, wsd_ref, o_ref, D=D, SD=SD)


def _final(yg, wcol, h2p, x1, mod_f, g_post, wsgu, wsd, out_prev, *, t0, tt, S):
    T, D = x1.shape
    tc = yg.shape[1]
    SD = wsd.shape[0]
    spt = S // tt
    i0 = t0 // tt
    assert t0 % tt == 0 and tc % tt == 0
    kern = functools.partial(_final_kernel, D=D, SD=SD)
    in_specs = [pl.BlockSpec((TOP_K, tt, D // 2), lambda i: (0, i, 0)),
                pl.BlockSpec((tt, TOP_K), lambda i: (i0 + i, 0)),
                pl.BlockSpec((tt, D // 2), lambda i: (i0 + i, 0)),
                pl.BlockSpec((tt, D), lambda i: (i0 + i, 0)),
                pl.BlockSpec((1, 3, D), lambda i: ((i0 + i) // spt, 0, 0)),
                pl.BlockSpec((1, D), lambda i: (0, 0)),
                pl.BlockSpec((D, 2 * SD), lambda i: (0, 0)),
                pl.BlockSpec((SD, D), lambda i: (0, 0))]
    args = [yg, wcol, h2p, x1, mod_f, g_post.reshape(1, D), wsgu, wsd]
    aliases = {}
    if out_prev is not None:
        in_specs.append(pl.BlockSpec(memory_space=pl.ANY))
        args.append(out_prev)
        aliases = {len(args) - 1: 0}
        kern = functools.partial(_final_kernel_chained, D=D, SD=SD)
    return pl.pallas_call(
        kern,
        grid=(tc // tt,),
        in_specs=in_specs,
        out_specs=pl.BlockSpec((tt, D), lambda i: (i0 + i, 0)),
        out_shape=jax.ShapeDtypeStruct((T, D), f32),
        input_output_aliases=aliases,
        compiler_params=pltpu.CompilerParams(dimension_semantics=("arbitrary",), vmem_limit_bytes=VMEM_LIMIT),
        name="final",
    )(*args)


def _layer(x, mod, g_pre_mix, g_post_mix, g_pre_ffn, g_post_ffn, w_in, w_dw, b_dw, ln_g, ln_b, w_conv_out,
           w_pool, b_pool, pool_scale, w_out, w_router, router_bias, w_gate_up, w_down, ws_gate_up, ws_down):
    B, S, D = x.shape
    T = B * S
    E = w_router.shape[1]
    mod6 = mod.reshape(B, 6, D)
    mod_m, mod_f = mod6[:, 0:3], mod6[:, 3:6]
    x1 = _mixer(x, mod_m, g_pre_mix, g_post_mix, w_in, w_dw, b_dw, ln_g, ln_b, w_conv_out, w_pool, b_pool,
                pool_scale, w_out, ts=min(512, S))
    h2p, eidx, rank, wcol, cnt = _route(x1, mod_f[:, 0:2], g_pre_ffn, w_router, router_bias, tt=min(512, S))
    TM = 512
    NT = (T * TOP_K) // TM + E
    dest, tstart = _plan(cnt, eidx, rank, TM=TM)
    xs = _dispatch(h2p, dest, NT * TM)
    ys = _experts(xs, tstart, w_gate_up, w_down, TM=TM)
    x1f = x1.reshape(T, D)
    wsgu, wsd = ws_gate_up.astype(bf16), ws_down.astype(bf16)
    tt = min(256, S)
    n_chunks = N_TAIL_CHUNKS
    while (T // n_chunks) % tt:
        n_chunks //= 2
    tc = T // n_chunks
    out = None
    for ci in range(n_chunks):
        yg = _collect(ys, dest, ci * tc, tc)
        out = _final(yg, wcol, h2p, x1f, mod_f, g_post_ffn, wsgu, wsd, out, t0=ci * tc, tt=tt, S=S)
    return out.reshape(B, S, D)


def kernel(x, c, w_ada, b_ada, g_pre_mix, g_post_mix, g_pre_ffn, g_post_ffn, w_in, w_dw, b_dw, ln_g, ln_b,
           w_conv_out, w_pool, b_pool, pool_scale, w_out, w_router, router_bias, w_gate_up, w_down, ws_gate_up,
           ws_down):
    depth = w_ada.shape[0]
    for l in range(depth):
        mod = _adaln(c, w_ada[l], b_ada[l])
        x = _layer(x, mod, g_pre_mix[l], g_post_mix[l], g_pre_ffn[l], g_post_ffn[l], w_in[l], w_dw[l], b_dw[l],
                   ln_g[l], ln_b[l], w_conv_out[l], w_pool[l], b_pool[l], pool_scale[l], w_out[l], w_router[l],
                   router_bias[l], w_gate_up[l], w_down[l], ws_gate_up[l], ws_down[l])
    return x
```

```python
import functools

import jax
import jax.numpy as jnp
from jax import lax
from jax.experimental import pallas as pl
from jax.experimental.pallas import tpu as pltpu
from jax.experimental.pallas import tpu_sc as plsc

EPS = 1e-6
TOP_K = 8
N_EXPERT_GROUPS = 8
TOPK_GROUPS = 4
ROUTED_SCALE = 2.5
POOL_WINDOWS = (2, 4, 8, 16)

LANES = 128
SUBLANES = 8
CONV_HIST = 32
POOL_HIST = 16
CONV_ROWS = 32
N_IN = 6
N_OUT = 4
N_SUB = 2
N_TAIL_CHUNKS = 8
SC_ROWS = 128
VMEM_LIMIT = 56 * 1024 * 1024

f32 = jnp.float32
bf16 = jnp.bfloat16
i32 = jnp.int32


def _sigmoid(v):
    return 0.5 * jnp.tanh(0.5 * v) + 0.5


def _silu(v):
    return v * _sigmoid(v)


def _pack_bf16_pair(lo, hi):
    lo_bits = lax.bitcast_convert_type(lo.astype(bf16).astype(f32), jnp.uint32)
    hi_bits = lax.bitcast_convert_type(hi.astype(bf16).astype(f32), jnp.uint32)
    word = lax.shift_right_logical(lo_bits, jnp.uint32(16)) | hi_bits
    return lax.bitcast_convert_type(word, i32)


def _unpack_bf16_pair(word):
    w = lax.bitcast_convert_type(word, jnp.uint32)
    lo = lax.bitcast_convert_type(lax.shift_left(w, jnp.uint32(16)), f32)
    hi = lax.bitcast_convert_type(w & jnp.uint32(0xFFFF0000), f32)
    return lo, hi


def _adaln_kernel(c_ref, w_ref, b_ref, o_ref):
    cond = _silu(c_ref[...])
    o_ref[...] = jnp.dot(cond.astype(bf16), w_ref[...].astype(bf16), preferred_element_type=f32) + b_ref[...]


def _adaln(c, w_ada, b_ada):
    B, D = c.shape
    N = w_ada.shape[1]
    tn = 1024
    return pl.pallas_call(
        _adaln_kernel,
        grid=(N // tn,),
        in_specs=[pl.BlockSpec((B, D), lambda j: (0, 0)),
                  pl.BlockSpec((D, tn), lambda j: (0, j)),
                  pl.BlockSpec((1, tn), lambda j: (0, j))],
        out_specs=pl.BlockSpec((B, tn), lambda j: (0, j)),
        out_shape=jax.ShapeDtypeStruct((B, N), f32),
        name="adaln",
    )(c, w_ada, b_ada.reshape(1, N))


def _mixer_kernel(x_ref, mod_ref, gpre_ref, gpost_ref, win_ref, wdw_ref, bdw_ref, lng_ref, lnb_ref,
                  wco_ref, wpool_ref, bpool_ref, pscale_ref, wout_ref, o_ref,
                  aext_ref, ashift_ref, pext_ref, act_ref, *, ts, D, conv_k):
    s = pl.program_id(1)

    @pl.when(s == 0)
    def _():
        aext_ref[0:CONV_HIST, :] = jnp.zeros((CONV_HIST, D), f32)
        pext_ref[0:POOL_HIST, :] = jnp.zeros((POOL_HIST, D), f32)

    x = x_ref[0]
    shift, scale, gate = mod_ref[0, 0:1, :], mod_ref[0, 1:2, :], mod_ref[0, 2:3, :]
    hb = (x * lax.rsqrt(jnp.mean(x * x, axis=-1, keepdims=True) + EPS) * (gpre_ref[...] * (1.0 + scale))
          + shift).astype(bf16)

    a_val = jnp.dot(hb, win_ref[:, 0:D], preferred_element_type=f32)
    a_gate = jnp.dot(hb, win_ref[:, D:2 * D], preferred_element_type=f32)
    aext_ref[CONV_HIST:CONV_HIST + ts, :] = a_val * _sigmoid(a_gate)
    n_ext = ts + CONV_HIST
    a_full = aext_ref[...]
    for r in range(1, SUBLANES):
        ashift_ref[r - 1] = pltpu.roll(a_full, n_ext - r, axis=0)
    rc = CONV_ROWS
    off0 = CONV_HIST - (conv_k - 1)
    for c in range(ts // rc):
        acc = jnp.broadcast_to(bdw_ref[...], (rc, D))
        for k in range(conv_k):
            r, q = (off0 + k) % SUBLANES, (off0 + k) // SUBLANES
            row0 = c * rc + q * SUBLANES
            tap = aext_ref[row0:row0 + rc, :] if r == 0 else ashift_ref[r - 1, row0:row0 + rc, :]
            acc = acc + jnp.concatenate([wdw_ref[k]] * (rc // SUBLANES), axis=0) * tap
        mu = jnp.mean(acc, axis=-1, keepdims=True)
        cen = acc - mu
        var = jnp.mean(cen * cen, axis=-1, keepdims=True)
        ln = cen * lax.rsqrt(var + EPS) * lng_ref[...] + lnb_ref[...]
        act_ref[c * rc:(c + 1) * rc, :] = _silu(ln).astype(bf16)
    aext_ref[0:CONV_HIST, :] = aext_ref[ts:ts + CONV_HIST, :]
    a_out = jnp.dot(act_ref[...], wco_ref[...], preferred_element_type=f32)

    pext_ref[POOL_HIST:POOL_HIST + ts, :] = jnp.dot(hb, win_ref[:, 2 * D:3 * D], preferred_element_type=f32)
    t_glob = s * ts + lax.broadcasted_iota(i32, (ts, 1), 0)
    dg = D // len(POOL_WINDOWS)
    bms = []
    for g, win in enumerate(POOL_WINDOWS):
        cols = slice(g * dg, (g + 1) * dg)
        p_ext = pext_ref[:, cols]
        wsum, j = p_ext, 1
        while j < win:
            wsum = wsum + pltpu.roll(wsum, j, axis=0)
            j *= 2
        cur = p_ext[POOL_HIST:, :]
        cnt = jnp.minimum(t_glob + 1, win).astype(f32)
        pooled = wsum[POOL_HIST:, :] / cnt - cur
        bms.append(jnp.dot(pooled.astype(bf16), wpool_ref[g], preferred_element_type=f32))
    pext_ref[0:POOL_HIST, :] = pext_ref[ts:ts + POOL_HIST, :]
    bm = (jnp.concatenate(bms, axis=-1) + bpool_ref[...]) * pscale_ref[...]

    g_a = _sigmoid(jnp.dot(hb, win_ref[:, 3 * D:4 * D], preferred_element_type=f32))
    g_b = _sigmoid(jnp.dot(hb, win_ref[:, 4 * D:5 * D], preferred_element_type=f32))
    mixed = (g_a * a_out + g_b * bm).astype(bf16)
    y = jnp.dot(mixed, wout_ref[...], preferred_element_type=f32)
    o_ref[0] = x + y * lax.rsqrt(jnp.mean(y * y, axis=-1, keepdims=True) + EPS) * (gpost_ref[...] * gate)


def _mixer(x, mod_m, g_pre, g_post, w_in, w_dw, b_dw, ln_g, ln_b, w_conv_out, w_pool, b_pool, pool_scale,
           w_out, *, ts):
    B, S, D = x.shape
    conv_k = w_dw.shape[0]
    assert conv_k - 1 <= CONV_HIST and max(POOL_WINDOWS) - 1 <= POOL_HIST
    assert S % ts == 0 and ts >= CONV_HIST and D % (LANES * len(POOL_WINDOWS)) == 0
    row = lambda v: v.reshape(1, D)
    vm = pl.BlockSpec(memory_space=pltpu.VMEM)
    kern = functools.partial(_mixer_kernel, ts=ts, D=D, conv_k=conv_k)
    return pl.pallas_call(
        kern,
        grid=(B, S // ts),
        in_specs=[pl.BlockSpec((1, ts, D), lambda b, s: (b, s, 0)),
                  pl.BlockSpec((1, 3, D), lambda b, s: (b, 0, 0)),
                  vm, vm, vm, vm, vm, vm, vm, vm, vm, vm, vm, vm],
        out_specs=pl.BlockSpec((1, ts, D), lambda b, s: (b, s, 0)),
        out_shape=jax.ShapeDtypeStruct((B, S, D), f32),
        scratch_shapes=[pltpu.VMEM((ts + CONV_HIST, D), f32),
                        pltpu.VMEM((SUBLANES - 1, ts + CONV_HIST, D), f32),
                        pltpu.VMEM((ts + POOL_HIST, D), f32),
                        pltpu.VMEM((ts, D), bf16)],
        compiler_params=pltpu.CompilerParams(dimension_semantics=("arbitrary", "arbitrary"),
                                             vmem_limit_bytes=VMEM_LIMIT),
        name="mixer",
    )(x, mod_m, row(g_pre), row(g_post), w_in.astype(bf16),
      jnp.broadcast_to(w_dw[:, None, :], (conv_k, SUBLANES, D)), row(b_dw), row(ln_g), row(ln_b),
      w_conv_out.astype(bf16), w_pool.astype(bf16), row(b_pool), row(pool_scale), w_out.astype(bf16))


def _route_kernel(x_ref, mod_ref, gpre_ref, wr_ref, bias_ref,
                  h2p_ref, eidx_ref, rank_ref, wcol_ref, cnt_ref, carry_ref, *, tt, D, E):
    i = pl.program_id(0)

    @pl.when(i == 0)
    def _():
        carry_ref[...] = jnp.zeros((E, LANES), f32)

    x = x_ref[...]
    shift, scale = mod_ref[0, 0:1, :], mod_ref[0, 1:2, :]
    h = x * lax.rsqrt(jnp.mean(x * x, axis=-1, keepdims=True) + EPS) * gpre_ref[...]
    h = h * (1.0 + scale) + shift
    h2p_ref[...] = _pack_bf16_pair(h[:, :D // 2], h[:, D // 2:])

    logits = lax.dot_general(wr_ref[...], h.astype(bf16), (((1,), (1,)), ((), ())),
                             preferred_element_type=f32)
    scores = _sigmoid(logits)
    choice = scores + bias_ref[...]
    G = N_EXPERT_GROUPS
    ge = E // G
    neg = jnp.float32(-jnp.inf)

    ch3 = choice.reshape(G, ge, tt)
    m1 = jnp.max(ch3, axis=1, keepdims=True)
    is_m1 = ch3 == m1
    n_m1 = jnp.sum(is_m1.astype(f32), axis=1, keepdims=True)
    m2 = jnp.max(jnp.where(is_m1, neg, ch3), axis=1, keepdims=True)
    gs = (m1 + jnp.where(n_m1 >= 2.0, m1, m2)).reshape(G, tt)

    gi = lax.broadcasted_iota(i32, (G, tt), 0).astype(f32)
    gsel = jnp.zeros((G, tt), f32)
    for _ in range(TOPK_GROUPS):
        gm = jnp.max(gs, axis=0, keepdims=True)
        first = jnp.min(jnp.where(gs == gm, gi, float(G)), axis=0, keepdims=True)
        hit = gi == first
        gsel = jnp.where(hit, 1.0, gsel)
        gs = jnp.where(hit, neg, gs)

    emask = jnp.broadcast_to(gsel.reshape(G, 1, tt), (G, ge, tt)).reshape(E, tt)
    masked = jnp.where(emask > 0.5, choice, neg)
    ei = lax.broadcasted_iota(i32, (E, tt), 0).astype(f32)
    sel = jnp.zeros((E, tt), f32)
    idx_rows, sc_rows = [], []
    for _ in range(TOP_K):
        mx = jnp.max(masked, axis=0, keepdims=True)
        first = jnp.min(jnp.where(masked == mx, ei, float(E)), axis=0, keepdims=True)
        hit = ei == first
        idx_rows.append(first)
        sc_rows.append(jnp.sum(jnp.where(hit, scores, 0.0), axis=0, keepdims=True))
        sel = sel + hit.astype(f32)
        masked = jnp.where(hit, neg, masked)

    upper = (lax.broadcasted_iota(i32, (tt, tt), 0) < lax.broadcasted_iota(i32, (tt, tt), 1)).astype(bf16)
    selb = sel.astype(bf16)
    excl = jnp.dot(selb, upper, preferred_element_type=f32)
    carry = carry_ref[...]
    pos = excl + jnp.concatenate([carry] * (tt // LANES), axis=1)
    carry = carry + jnp.dot(selb, jnp.ones((tt, LANES), bf16), preferred_element_type=f32)
    carry_ref[...] = carry
    cnt_ref[...] = carry

    ssum = sc_rows[0]
    for r in sc_rows[1:]:
        ssum = ssum + r
    wscale = ROUTED_SCALE / ssum
    nb = tt // LANES
    for k in range(TOP_K):
        rk = jnp.sum(jnp.where(ei == idx_rows[k], pos, 0.0), axis=0, keepdims=True).astype(i32)
        for j in range(nb):
            eidx_ref[j, k:k + 1, :] = idx_rows[k][:, j * LANES:(j + 1) * LANES].astype(i32)
            rank_ref[j, k:k + 1, :] = rk[:, j * LANES:(j + 1) * LANES]
    wstack = jnp.concatenate([r * wscale for r in sc_rows] + [jnp.zeros((LANES - TOP_K, tt), f32)], axis=0)
    for j in range(nb):
        blk = wstack[:, j * LANES:(j + 1) * LANES].T
        wcol_ref[j * LANES:(j + 1) * LANES, :] = blk[:, 0:TOP_K]


def _route(x1, mod_f, g_pre, w_router, router_bias, *, tt):
    B, S, D = x1.shape
    T = B * S
    E = w_router.shape[1]
    assert S % tt == 0 and tt % LANES == 0 and E % (N_EXPERT_GROUPS * SUBLANES) == 0
    spt = S // tt
    nb = tt // LANES
    kern = functools.partial(_route_kernel, tt=tt, D=D, E=E)
    return pl.pallas_call(
        kern,
        grid=(T // tt,),
        in_specs=[pl.BlockSpec((tt, D), lambda i: (i, 0)),
                  pl.BlockSpec((1, 2, D), lambda i: (i // spt, 0, 0)),
                  pl.BlockSpec((1, D), lambda i: (0, 0)),
                  pl.BlockSpec((E, D), lambda i: (0, 0)),
                  pl.BlockSpec((E, 1), lambda i: (0, 0))],
        out_specs=[pl.BlockSpec((tt, D // 2), lambda i: (i, 0)),
                   pl.BlockSpec((nb, TOP_K, LANES), lambda i: (i, 0, 0)),
                   pl.BlockSpec((nb, TOP_K, LANES), lambda i: (i, 0, 0)),
                   pl.BlockSpec((tt, TOP_K), lambda i: (i, 0)),
                   pl.BlockSpec((E, LANES), lambda i: (0, 0))],
        out_shape=[jax.ShapeDtypeStruct((T, D // 2), i32),
                   jax.ShapeDtypeStruct((T // LANES, TOP_K, LANES), i32),
                   jax.ShapeDtypeStruct((T // LANES, TOP_K, LANES), i32),
                   jax.ShapeDtypeStruct((T, TOP_K), f32),
                   jax.ShapeDtypeStruct((E, LANES), f32)],
        scratch_shapes=[pltpu.VMEM((E, LANES), f32)],
        compiler_params=pltpu.CompilerParams(dimension_semantics=("arbitrary",), vmem_limit_bytes=VMEM_LIMIT),
        name="route",
    )(x1.reshape(T, D), mod_f, g_pre.reshape(1, D), w_router.T.astype(bf16), router_bias.reshape(E, 1))


def _plan_kernel(cnt_ref, eidx_ref, rank_ref, dest_ref, tstart_ref, total_ref, pstart_ref, *, E, TM, nb):
    i = pl.program_id(0)

    @pl.when(i == 0)
    def _():
        cnt_col = cnt_ref[:, 0:1]
        pad_col = jnp.ceil(cnt_col / TM) * TM
        r = lax.broadcasted_iota(i32, (E, E), 0)
        c = lax.broadcasted_iota(i32, (E, E), 1)
        pad_row = jnp.sum(jnp.where(r == c, pad_col, 0.0), axis=0, keepdims=True)
        pstart = jnp.sum(jnp.where(c < r, pad_row, 0.0), axis=1, keepdims=True)
        pstart_ref[...] = jnp.broadcast_to(pstart, (E, LANES))
        pstart_row = jnp.sum(jnp.where(r == c, pstart, 0.0), axis=0, keepdims=True)
        tstart_ref[...] = (pstart_row / TM).astype(i32)
        total = jnp.sum(pad_col, axis=0, keepdims=True)
        total_ref[...] = jnp.broadcast_to(total / TM, (1, LANES)).astype(i32)

    ei = lax.broadcasted_iota(i32, (E, LANES), 0)
    pstart = pstart_ref[...]
    for j in range(nb):
        for k in range(TOP_K):
            e_row = eidx_ref[j, k:k + 1, :]
            base = jnp.sum(jnp.where(ei == e_row, pstart, 0.0), axis=0, keepdims=True).astype(i32)
            dest_ref[j, k:k + 1, :] = base + rank_ref[j, k:k + 1, :]


def _plan(cnt, eidx, rank, *, TM):
    E = cnt.shape[0]
    NB = eidx.shape[0]
    nb = min(16, NB)
    assert NB % nb == 0
    kern = functools.partial(_plan_kernel, E=E, TM=TM, nb=nb)
    blk = pl.BlockSpec((nb, TOP_K, LANES), lambda i: (i, 0, 0))
    dest, tstart, total = pl.pallas_call(
        kern,
        grid=(NB // nb,),
        in_specs=[pl.BlockSpec((E, LANES), lambda i: (0, 0)), blk, blk],
        out_specs=[blk,
                   pl.BlockSpec((1, E), lambda i: (0, 0)),
                   pl.BlockSpec((1, LANES), lambda i: (0, 0))],
        out_shape=[jax.ShapeDtypeStruct((NB, TOP_K, LANES), i32),
                   jax.ShapeDtypeStruct((1, E), i32),
                   jax.ShapeDtypeStruct((1, LANES), i32)],
        scratch_shapes=[pltpu.VMEM((E, LANES), f32)],
        compiler_params=pltpu.CompilerParams(dimension_semantics=("arbitrary",)),
        name="plan",
    )(cnt, eidx, rank)
    return dest, jnp.concatenate([tstart.reshape(E), total.reshape(LANES)[0:1]])


def _sc_workers():
    info = plsc.get_sparse_core_info()
    return info.num_cores, info.num_subcores


def _dispatch(h2p, dest, NP):
    T, W = h2p.shape
    NC, NS = _sc_workers()
    nblk = T // (NC * NS * SC_ROWS)
    assert T == nblk * NC * NS * SC_ROWS
    mesh = plsc.VectorSubcoreMesh(core_axis_name="c", subcore_axis_name="s")

    @functools.partial(
        pl.kernel, mesh=mesh, out_type=jax.ShapeDtypeStruct((NP, W), h2p.dtype),
        scratch_types=[pltpu.VMEM((TOP_K, SC_ROWS), i32), pltpu.VMEM((SC_ROWS, W), h2p.dtype)],
        name="dispatch")
    def kern(h_hbm, dest_hbm, xs_hbm, idx_v, rows_v):
        wid = lax.axis_index("s") * NC + lax.axis_index("c")

        @pl.loop(0, nblk)
        def _(b):
            blk = wid * nblk + b
            pltpu.sync_copy(dest_hbm.at[blk], idx_v)
            pltpu.sync_copy(h_hbm.at[pl.ds(pl.multiple_of(blk * SC_ROWS, SC_ROWS), SC_ROWS)], rows_v)
            for k in range(TOP_K):
                pltpu.sync_copy(rows_v, xs_hbm.at[idx_v.at[k]])

    return kern(h2p, dest)


def _collect(ys, dest, t0, tc):
    NP, W = ys.shape
    NC, NS = _sc_workers()
    nblk = tc // (NC * NS * SC_ROWS)
    assert tc == nblk * NC * NS * SC_ROWS and t0 % SC_ROWS == 0
    blk0 = t0 // SC_ROWS
    mesh = plsc.VectorSubcoreMesh(core_axis_name="c", subcore_axis_name="s")

    @functools.partial(
        pl.kernel, mesh=mesh, out_type=jax.ShapeDtypeStruct((TOP_K, tc, W), ys.dtype),
        scratch_types=[pltpu.VMEM((TOP_K, SC_ROWS), i32), pltpu.VMEM((SC_ROWS, W), ys.dtype)],
        name="collect")
    def kern(ys_hbm, dest_hbm, yg_hbm, idx_v, rows_v):
        wid = lax.axis_index("s") * NC + lax.axis_index("c")

        @pl.loop(0, nblk)
        def _(b):
            blk = wid * nblk + b
            pltpu.sync_copy(dest_hbm.at[blk0 + blk], idx_v)
            for k in range(TOP_K):
                pltpu.sync_copy(ys_hbm.at[idx_v.at[k]], rows_v)
                pltpu.sync_copy(rows_v, yg_hbm.at[k, pl.ds(pl.multiple_of(blk * SC_ROWS, SC_ROWS), SC_ROWS)])

    return kern(ys, dest)


def _experts_kernel(tstart_ref, xs_hbm, wgu_ref, wd_ref, ys_hbm, xbuf, ybuf, actbuf, state, sem_in, sem_out,
                    wgu_bf, wd_bf, *, TM, ED, D):
    e = pl.program_id(0)
    n_exp = pl.num_programs(0)
    u_lo, u_hi, n_tiles = tstart_ref[e], tstart_ref[e + 1], tstart_ref[n_exp]
    sub = TM // N_SUB

    def in_copy(u, slot):
        rows = pl.ds(pl.multiple_of(u * TM, TM), TM)
        return pltpu.make_async_copy(xs_hbm.at[rows], xbuf.at[slot], sem_in.at[slot])

    def out_copy(u, slot):
        rows = pl.ds(pl.multiple_of(u * TM, TM), TM)
        return pltpu.make_async_copy(ybuf.at[slot], ys_hbm.at[rows], sem_out.at[slot])

    def wait_in(u):
        in_copy(u, lax.rem(u, N_IN)).wait()

    def start_in_ahead(u):
        ahead = u + (N_IN - 1)

        @pl.when(ahead < n_tiles)
        def _():
            in_copy(ahead, lax.rem(ahead, N_IN)).start()

    def wait_out_slot(u):
        @pl.when(u >= N_OUT)
        def _():
            out_copy(u - N_OUT, lax.rem(u, N_OUT)).wait()

    def start_out(u):
        out_copy(u, lax.rem(u, N_OUT)).start()

    def up(u):
        slot, aslot = lax.rem(u, N_IN), lax.rem(u, 2)
        for j in range(N_SUB):
            rows = slice(j * sub, (j + 1) * sub)
            lo, hi = _unpack_bf16_pair(xbuf[slot, rows, :])
            gu = (jnp.dot(lo.astype(bf16), wgu_bf[0:D // 2, :], preferred_element_type=f32)
                  + jnp.dot(hi.astype(bf16), wgu_bf[D // 2:, :], preferred_element_type=f32))
            actbuf[aslot, rows, :] = (_silu(gu[:, :ED]) * gu[:, ED:]).astype(bf16)

    def down(u, wslot):
        oslot, aslot = lax.rem(u, N_OUT), lax.rem(u, 2)
        for j in range(N_SUB):
            rows = slice(j * sub, (j + 1) * sub)
            y = jnp.dot(actbuf[aslot, rows, :], wd_bf[wslot], preferred_element_type=f32)
            ybuf[oslot, rows, :] = _pack_bf16_pair(y[:, :D // 2], y[:, D // 2:])

    @pl.when(e == 0)
    def _():
        state[0] = 0
        state[1] = 0
        for u in range(N_IN - 1):
            @pl.when(u < n_tiles)
            def _():
                in_copy(u, u).start()

    @pl.when(u_hi > u_lo)
    def _():
        old_w = state[1]
        new_w = 1 - old_w

        @pl.when(state[0] == 1)
        def _():
            wait_in(u_lo)
            wait_out_slot(u_lo - 1)
            down(u_lo - 1, old_w)
            wgu_bf[...] = wgu_ref[0].astype(bf16)
            wd_bf[new_w] = wd_ref[0].astype(bf16)
            up(u_lo)
            start_out(u_lo - 1)
            start_in_ahead(u_lo)

        @pl.when(state[0] == 0)
        def _():
            wait_in(u_lo)
            wgu_bf[...] = wgu_ref[0].astype(bf16)
            wd_bf[new_w] = wd_ref[0].astype(bf16)
            up(u_lo)
            start_in_ahead(u_lo)

        def skewed(u, carry):
            wait_in(u)
            wait_out_slot(u - 1)
            down(u - 1, new_w)
            up(u)
            start_out(u - 1)
            start_in_ahead(u)
            return carry

        n_rest = u_hi - (u_lo + 1)

        def skewed_pair(i, carry):
            u = u_lo + 1 + 2 * i
            wait_in(u)
            wait_in(u + 1)
            wait_out_slot(u - 1)
            wait_out_slot(u)
            start_in_ahead(u)
            down(u - 1, new_w)
            up(u)
            down(u, new_w)
            up(u + 1)
            start_out(u - 1)
            start_out(u)
            start_in_ahead(u + 1)
            return carry

        lax.fori_loop(0, lax.shift_right_logical(n_rest, 1), skewed_pair, 0)

        @pl.when(lax.rem(n_rest, 2) == 1)
        def _():
            skewed(u_hi - 1, 0)

        state[0] = 1
        state[1] = new_w

    @pl.when(e == n_exp - 1)
    def _():
        @pl.when(state[0] == 1)
        def _():
            wait_out_slot(n_tiles - 1)
            down(n_tiles - 1, state[1])
            start_out(n_tiles - 1)

        for back in range(N_OUT, 0, -1):
            @pl.when(n_tiles >= back)
            def _():
                out_copy(n_tiles - back, lax.rem(n_tiles - back, N_OUT)).wait()


def _experts(xs, tstart, w_gate_up, w_down, *, TM):
    NP, W = xs.shape
    E, D, ED2 = w_gate_up.shape
    ED = ED2 // 2
    kern = functools.partial(_experts_kernel, TM=TM, ED=ED, D=D)
    return pl.pallas_call(
        kern,
        grid_spec=pltpu.PrefetchScalarGridSpec(
            num_scalar_prefetch=1,
            grid=(E,),
            in_specs=[pl.BlockSpec(memory_space=pl.ANY),
                      pl.BlockSpec((1, D, ED2), lambda e, ts: (e, 0, 0)),
                      pl.BlockSpec((1, ED, D), lambda e, ts: (e, 0, 0))],
            out_specs=pl.BlockSpec(memory_space=pl.ANY),
            scratch_shapes=[pltpu.VMEM((N_IN, TM, W), i32), pltpu.VMEM((N_OUT, TM, W), i32),
                            pltpu.VMEM((2, TM, ED), bf16), pltpu.SMEM((2,), i32),
                            pltpu.SemaphoreType.DMA((N_IN,)), pltpu.SemaphoreType.DMA((N_OUT,)),
                            pltpu.VMEM((D, ED2), bf16), pltpu.VMEM((2, ED, D), bf16)]),
        out_shape=jax.ShapeDtypeStruct((NP, W), i32),
        compiler_params=pltpu.CompilerParams(dimension_semantics=("arbitrary",), vmem_limit_bytes=VMEM_LIMIT),
        name="experts",
    )(tstart, xs, w_gate_up, w_down)


def _final_kernel(yg_ref, wcol_ref, h2p_ref, x1_ref, mod_ref, gpost_ref, wsgu_ref, wsd_ref, o_ref, *, D, SD):
    wcol = wcol_ref[...]
    r_lo = jnp.zeros(h2p_ref.shape, f32)
    r_hi = jnp.zeros(h2p_ref.shape, f32)
    for k in range(TOP_K):
        lo, hi = _unpack_bf16_pair(yg_ref[k])
        wk = wcol[:, k:k + 1]
        r_lo = r_lo + wk * lo
        r_hi = r_hi + wk * hi
    routed = jnp.concatenate([r_lo, r_hi], axis=-1)

    hlo, hhi = _unpack_bf16_pair(h2p_ref[...])
    gs = (jnp.dot(hlo.astype(bf16), wsgu_ref[0:D // 2, :], preferred_element_type=f32)
          + jnp.dot(hhi.astype(bf16), wsgu_ref[D // 2:, :], preferred_element_type=f32))
    sh = (_silu(gs[:, :SD]) * gs[:, SD:]).astype(bf16)
    y = routed + jnp.dot(sh, wsd_ref[...], preferred_element_type=f32)
    yn = y * lax.rsqrt(jnp.mean(y * y, axis=-1, keepdims=True) + EPS) * gpost_ref[...]
    o_ref[...] = x1_ref[...] + mod_ref[0, 2:3, :] * yn


def _final_kernel_chained(yg_ref, wcol_ref, h2p_ref, x1_ref, mod_ref, gpost_ref, wsgu_ref, wsd_ref, prev_ref, o_ref,
                          *, D, SD):
    del prev_ref
    _final_kernel(yg_ref, wcol_ref, h2p_ref, x1_ref, mod_ref, gpost_ref, wsgu_ref, wsd_ref, o_ref, D=D, SD=SD)


def _final(yg, wcol, h2p, x1, mod_f, g_post, wsgu, wsd, out_prev, *, t0, tt, S):
    T, D = x1.shape
    tc = yg.shape[1]
    SD = wsd.shape[0]
    spt = S // tt
    i0 = t0 // tt
    assert t0 % tt == 0 and tc % tt == 0
    kern = functools.partial(_final_kernel, D=D, SD=SD)
    in_specs = [pl.BlockSpec((TOP_K, tt, D // 2), lambda i: (0, i, 0)),
                pl.BlockSpec((tt, TOP_K), lambda i: (i0 + i, 0)),
                pl.BlockSpec((tt, D // 2), lambda i: (i0 + i, 0)),
                pl.BlockSpec((tt, D), lambda i: (i0 + i, 0)),
                pl.BlockSpec((1, 3, D), lambda i: ((i0 + i) // spt, 0, 0)),
                pl.BlockSpec((1, D), lambda i: (0, 0)),
                pl.BlockSpec((D, 2 * SD), lambda i: (0, 0)),
                pl.BlockSpec((SD, D), lambda i: (0, 0))]
    args = [yg, wcol, h2p, x1, mod_f, g_post.reshape(1, D), wsgu, wsd]
    aliases = {}
    if out_prev is not None:
        in_specs.append(pl.BlockSpec(memory_space=pl.ANY))
        args.append(out_prev)
        aliases = {len(args) - 1: 0}
        kern = functools.partial(_final_kernel_chained, D=D, SD=SD)
    return pl.pallas_call(
        kern,
        grid=(tc // tt,),
        in_specs=in_specs,
        out_specs=pl.BlockSpec((tt, D), lambda i: (i0 + i, 0)),
        out_shape=jax.ShapeDtypeStruct((T, D), f32),
        input_output_aliases=aliases,
        compiler_params=pltpu.CompilerParams(dimension_semantics=("arbitrary",), vmem_limit_bytes=VMEM_LIMIT),
        name="final",
    )(*args)


def _layer(x, mod, g_pre_mix, g_post_mix, g_pre_ffn, g_post_ffn, w_in, w_dw, b_dw, ln_g, ln_b, w_conv_out,
           w_pool, b_pool, pool_scale, w_out, w_router, router_bias, w_gate_up, w_down, ws_gate_up, ws_down):
    B, S, D = x.shape
    T = B * S
    E = w_router.shape[1]
    mod6 = mod.reshape(B, 6, D)
    mod_m, mod_f = mod6[:, 0:3], mod6[:, 3:6]
    x1 = _mixer(x, mod_m, g_pre_mix, g_post_mix, w_in, w_dw, b_dw, ln_g, ln_b, w_conv_out, w_pool, b_pool,
                pool_scale, w_out, ts=min(512, S))
    h2p, eidx, rank, wcol, cnt = _route(x1, mod_f[:, 0:2], g_pre_ffn, w_router, router_bias, tt=min(512, S))
    TM = 512
    NT = (T * TOP_K) // TM + E
    dest, tstart = _plan(cnt, eidx, rank, TM=TM)
    xs = _dispatch(h2p, dest, NT * TM)
    ys = _experts(xs, tstart, w_gate_up, w_down, TM=TM)
    x1f = x1.reshape(T, D)
    wsgu, wsd = ws_gate_up.astype(bf16), ws_down.astype(bf16)
    tt = min(256, S)
    n_chunks = N_TAIL_CHUNKS
    while (T // n_chunks) % tt:
        n_chunks //= 2
    tc = T // n_chunks
    out = None
    for ci in range(n_chunks):
        yg = _collect(ys, dest, ci * tc, tc)
        out = _final(yg, wcol, h2p, x1f, mod_f, g_post_ffn, wsgu, wsd, out, t0=ci * tc, tt=tt, S=S)
    return out.reshape(B, S, D)


def kernel(x, c, w_ada, b_ada, g_pre_mix, g_post_mix, g_pre_ffn, g_post_ffn, w_in, w_dw, b_dw, ln_g, ln_b,
           w_conv_out, w_pool, b_pool, pool_scale, w_out, w_router, router_bias, w_gate_up, w_down, ws_gate_up,
           ws_down):
    depth = w_ada.shape[0]
    for l in range(depth):
        mod = _adaln(c, w_ada[l], b_ada[l])
        x = _layer(x, mod, g_pre_mix[l], g_post_mix[l], g_pre_ffn[l], g_post_ffn[l], w_in[l], w_dw[l], b_dw[l],
                   ln_g[l], ln_b[l], w_conv_out[l], w_pool[l], b_pool[l], pool_scale[l], w_out[l], w_router[l],
                   router_bias[l], w_gate_up[l], w_down[l], ws_gate_up[l], ws_down[l])
    return x
```

```python
import functools

import jax
import jax.numpy as jnp
from jax import lax
from jax.experimental import pallas as pl
from jax.experimental.pallas import tpu as pltpu
from jax.experimental.pallas import tpu_sc as plsc

EPS = 1e-6
TOP_K = 8
N_EXPERT_GROUPS = 8
TOPK_GROUPS = 4
ROUTED_SCALE = 2.5
POOL_WINDOWS = (2, 4, 8, 16)

LANES = 128
SUBLANES = 8
CONV_HIST = 32
POOL_HIST = 16
CONV_ROWS = 32
N_IN = 6
N_OUT = 4
N_SUB = 2
N_TAIL_CHUNKS = 8
SC_ROWS = 128
VMEM_LIMIT = 56 * 1024 * 1024

f32 = jnp.float32
bf16 = jnp.bfloat16
i32 = jnp.int32


def _sigmoid(v):
    return 0.5 * jnp.tanh(0.5 * v) + 0.5


def _sigmoid_of_half(hv):
    return 0.5 * jnp.tanh(hv) + 0.5


def _silu(v):
    return v * _sigmoid(v)


def _pack_bf16_pair(lo, hi):
    lo_bits = lax.bitcast_convert_type(lo.astype(bf16).astype(f32), jnp.uint32)
    hi_bits = lax.bitcast_convert_type(hi.astype(bf16).astype(f32), jnp.uint32)
    word = lax.shift_right_logical(lo_bits, jnp.uint32(16)) | hi_bits
    return lax.bitcast_convert_type(word, i32)


def _unpack_bf16_pair(word):
    w = lax.bitcast_convert_type(word, jnp.uint32)
    lo = lax.bitcast_convert_type(lax.shift_left(w, jnp.uint32(16)), f32)
    hi = lax.bitcast_convert_type(w & jnp.uint32(0xFFFF0000), f32)
    return lo, hi


def _adaln_kernel(c_ref, w_ref, b_ref, o_ref):
    cond = _silu(c_ref[...])
    o_ref[...] = jnp.dot(cond.astype(bf16), w_ref[...].astype(bf16), preferred_element_type=f32) + b_ref[...]


def _adaln(c, w_ada, b_ada):
    B, D = c.shape
    N = w_ada.shape[1]
    tn = 1024
    return pl.pallas_call(
        _adaln_kernel,
        grid=(N // tn,),
        in_specs=[pl.BlockSpec((B, D), lambda j: (0, 0)),
                  pl.BlockSpec((D, tn), lambda j: (0, j)),
                  pl.BlockSpec((1, tn), lambda j: (0, j))],
        out_specs=pl.BlockSpec((B, tn), lambda j: (0, j)),
        out_shape=jax.ShapeDtypeStruct((B, N), f32),
        name="adaln",
    )(c, w_ada, b_ada.reshape(1, N))


def _mixer_kernel(x_ref, mod_ref, gpre_ref, gpost_ref, win_ref, wdw_ref, bdw_ref, lng_ref, lnb_ref,
                  wco_ref, wpool_ref, bpool_ref, pscale_ref, wout_ref, o_ref,
                  aext_ref, ashift_ref, pext_ref, act_ref, *, ts, D, conv_k):
    s = pl.program_id(1)

    @pl.when(s == 0)
    def _():
        aext_ref[0:CONV_HIST, :] = jnp.zeros((CONV_HIST, D), f32)
        pext_ref[0:POOL_HIST, :] = jnp.zeros((POOL_HIST, D), f32)

    x = x_ref[0]
    shift, scale, gate = mod_ref[0, 0:1, :], mod_ref[0, 1:2, :], mod_ref[0, 2:3, :]
    hb = (x * lax.rsqrt(jnp.mean(x * x, axis=-1, keepdims=True) + EPS) * (gpre_ref[...] * (1.0 + scale))
          + shift).astype(bf16)

    a_val = jnp.dot(hb, win_ref[:, 0:D], preferred_element_type=f32)
    a_gate = jnp.dot(hb, win_ref[:, D:2 * D], preferred_element_type=f32)
    aext_ref[CONV_HIST:CONV_HIST + ts, :] = a_val * _sigmoid_of_half(a_gate)
    n_ext = ts + CONV_HIST
    a_full = aext_ref[...]
    for r in range(1, SUBLANES):
        ashift_ref[r - 1] = pltpu.roll(a_full, n_ext - r, axis=0)
    rc = CONV_ROWS
    off0 = CONV_HIST - (conv_k - 1)
    for c in range(ts // rc):
        acc = jnp.broadcast_to(bdw_ref[...], (rc, D))
        for k in range(conv_k):
            r, q = (off0 + k) % SUBLANES, (off0 + k) // SUBLANES
            row0 = c * rc + q * SUBLANES
            tap = aext_ref[row0:row0 + rc, :] if r == 0 else ashift_ref[r - 1, row0:row0 + rc, :]
            acc = acc + jnp.concatenate([wdw_ref[k]] * (rc // SUBLANES), axis=0) * tap
        mu = jnp.mean(acc, axis=-1, keepdims=True)
        cen = acc - mu
        var = jnp.mean(cen * cen, axis=-1, keepdims=True)
        ln = cen * lax.rsqrt(var + EPS) * lng_ref[...] + lnb_ref[...]
        act_ref[c * rc:(c + 1) * rc, :] = _silu(ln).astype(bf16)
    aext_ref[0:CONV_HIST, :] = aext_ref[ts:ts + CONV_HIST, :]
    a_out = jnp.dot(act_ref[...], wco_ref[...], preferred_element_type=f32)

    pext_ref[POOL_HIST:POOL_HIST + ts, :] = jnp.dot(hb, win_ref[:, 2 * D:3 * D], preferred_element_type=f32)
    t_glob = s * ts + lax.broadcasted_iota(i32, (ts, 1), 0)
    dg = D // len(POOL_WINDOWS)
    bms = []
    for g, win in enumerate(POOL_WINDOWS):
        cols = slice(g * dg, (g + 1) * dg)
        p_ext = pext_ref[:, cols]
        wsum, j = p_ext, 1
        while j < win:
            wsum = wsum + pltpu.roll(wsum, j, axis=0)
            j *= 2
        cur = p_ext[POOL_HIST:, :]
        cnt = jnp.minimum(t_glob + 1, win).astype(f32)
        pooled = wsum[POOL_HIST:, :] / cnt - cur
        bms.append(jnp.dot(pooled.astype(bf16), wpool_ref[g], preferred_element_type=f32))
    pext_ref[0:POOL_HIST, :] = pext_ref[ts:ts + POOL_HIST, :]
    bm = (jnp.concatenate(bms, axis=-1) + bpool_ref[...]) * pscale_ref[...]

    g_a = _sigmoid_of_half(jnp.dot(hb, win_ref[:, 3 * D:4 * D], preferred_element_type=f32))
    g_b = _sigmoid_of_half(jnp.dot(hb, win_ref[:, 4 * D:5 * D], preferred_element_type=f32))
    mixed = (g_a * a_out + g_b * bm).astype(bf16)
    y = jnp.dot(mixed, wout_ref[...], preferred_element_type=f32)
    o_ref[0] = x + y * lax.rsqrt(jnp.mean(y * y, axis=-1, keepdims=True) + EPS) * (gpost_ref[...] * gate)


def _mixer(x, mod_m, g_pre, g_post, w_in, w_dw, b_dw, ln_g, ln_b, w_conv_out, w_pool, b_pool, pool_scale,
           w_out, *, ts):
    B, S, D = x.shape
    conv_k = w_dw.shape[0]
    assert conv_k - 1 <= CONV_HIST and max(POOL_WINDOWS) - 1 <= POOL_HIST
    assert S % ts == 0 and ts >= CONV_HIST and D % (LANES * len(POOL_WINDOWS)) == 0
    row = lambda v: v.reshape(1, D)
    vm = pl.BlockSpec(memory_space=pltpu.VMEM)
    half_cols = jnp.concatenate([jnp.ones((D,), f32), jnp.full((D,), 0.5, f32), jnp.ones((D,), f32),
                                 jnp.full((2 * D,), 0.5, f32)])[None, :]
    kern = functools.partial(_mixer_kernel, ts=ts, D=D, conv_k=conv_k)
    return pl.pallas_call(
        kern,
        grid=(B, S // ts),
        in_specs=[pl.BlockSpec((1, ts, D), lambda b, s: (b, s, 0)),
                  pl.BlockSpec((1, 3, D), lambda b, s: (b, 0, 0)),
                  vm, vm, vm, vm, vm, vm, vm, vm, vm, vm, vm, vm],
        out_specs=pl.BlockSpec((1, ts, D), lambda b, s: (b, s, 0)),
        out_shape=jax.ShapeDtypeStruct((B, S, D), f32),
        scratch_shapes=[pltpu.VMEM((ts + CONV_HIST, D), f32),
                        pltpu.VMEM((SUBLANES - 1, ts + CONV_HIST, D), f32),
                        pltpu.VMEM((ts + POOL_HIST, D), f32),
                        pltpu.VMEM((ts, D), bf16)],
        compiler_params=pltpu.CompilerParams(dimension_semantics=("arbitrary", "arbitrary"),
                                             vmem_limit_bytes=VMEM_LIMIT),
        name="mixer",
    )(x, mod_m, row(g_pre), row(g_post), (w_in * half_cols).astype(bf16),
      jnp.broadcast_to(w_dw[:, None, :], (conv_k, SUBLANES, D)), row(b_dw), row(ln_g), row(ln_b),
      w_conv_out.astype(bf16), w_pool.astype(bf16), row(b_pool), row(pool_scale), w_out.astype(bf16))


def _route_kernel(x_ref, mod_ref, gpre_ref, wr_ref, bias_ref,
                  h2p_ref, eidx_ref, rank_ref, wcol_ref, cnt_ref, carry_ref, *, tt, D, E):
    i = pl.program_id(0)

    @pl.when(i == 0)
    def _():
        carry_ref[...] = jnp.zeros((E, LANES), f32)

    x = x_ref[...]
    shift, scale = mod_ref[0, 0:1, :], mod_ref[0, 1:2, :]
    h = x * lax.rsqrt(jnp.mean(x * x, axis=-1, keepdims=True) + EPS) * gpre_ref[...]
    h = h * (1.0 + scale) + shift
    h2p_ref[...] = _pack_bf16_pair(h[:, :D // 2], h[:, D // 2:])

    logits = lax.dot_general(wr_ref[...], h.astype(bf16), (((1,), (1,)), ((), ())),
                             preferred_element_type=f32)
    scores = _sigmoid(logits)
    choice = scores + bias_ref[...]
    G = N_EXPERT_GROUPS
    ge = E // G
    neg = jnp.float32(-jnp.inf)

    ch3 = choice.reshape(G, ge, tt)
    m1 = jnp.max(ch3, axis=1, keepdims=True)
    is_m1 = ch3 == m1
    n_m1 = jnp.sum(is_m1.astype(f32), axis=1, keepdims=True)
    m2 = jnp.max(jnp.where(is_m1, neg, ch3), axis=1, keepdims=True)
    gs = (m1 + jnp.where(n_m1 >= 2.0, m1, m2)).reshape(G, tt)

    gi = lax.broadcasted_iota(i32, (G, tt), 0).astype(f32)
    gsel = jnp.zeros((G, tt), f32)
    for _ in range(TOPK_GROUPS):
        gm = jnp.max(gs, axis=0, keepdims=True)
        first = jnp.min(jnp.where(gs == gm, gi, float(G)), axis=0, keepdims=True)
        hit = gi == first
        gsel = jnp.where(hit, 1.0, gsel)
        gs = jnp.where(hit, neg, gs)

    emask = jnp.broadcast_to(gsel.reshape(G, 1, tt), (G, ge, tt)).reshape(E, tt)
    masked = jnp.where(emask > 0.5, choice, neg)
    ei = lax.broadcasted_iota(i32, (E, tt), 0).astype(f32)
    sel = jnp.zeros((E, tt), f32)
    idx_rows, sc_rows = [], []
    for _ in range(TOP_K):
        mx = jnp.max(masked, axis=0, keepdims=True)
        first = jnp.min(jnp.where(masked == mx, ei, float(E)), axis=0, keepdims=True)
        hit = ei == first
        idx_rows.append(first)
        sc_rows.append(jnp.sum(jnp.where(hit, scores, 0.0), axis=0, keepdims=True))
        sel = sel + hit.astype(f32)
        masked = jnp.where(hit, neg, masked)

    upper = (lax.broadcasted_iota(i32, (tt, tt), 0) < lax.broadcasted_iota(i32, (tt, tt), 1)).astype(bf16)
    selb = sel.astype(bf16)
    excl = jnp.dot(selb, upper, preferred_element_type=f32)
    carry = carry_ref[...]
    pos = excl + jnp.concatenate([carry] * (tt // LANES), axis=1)
    carry = carry + jnp.dot(selb, jnp.ones((tt, LANES), bf16), preferred_element_type=f32)
    carry_ref[...] = carry
    cnt_ref[...] = carry

    ssum = sc_rows[0]
    for r in sc_rows[1:]:
        ssum = ssum + r
    wscale = ROUTED_SCALE / ssum
    nb = tt // LANES
    for k in range(TOP_K):
        rk = jnp.sum(jnp.where(ei == idx_rows[k], pos, 0.0), axis=0, keepdims=True).astype(i32)
        for j in range(nb):
            eidx_ref[j, k:k + 1, :] = idx_rows[k][:, j * LANES:(j + 1) * LANES].astype(i32)
            rank_ref[j, k:k + 1, :] = rk[:, j * LANES:(j + 1) * LANES]
    wstack = jnp.concatenate([r * wscale for r in sc_rows] + [jnp.zeros((LANES - TOP_K, tt), f32)], axis=0)
    for j in range(nb):
        blk = wstack[:, j * LANES:(j + 1) * LANES].T
        wcol_ref[j * LANES:(j + 1) * LANES, :] = blk[:, 0:TOP_K]


def _route(x1, mod_f, g_pre, w_router, router_bias, *, tt):
    B, S, D = x1.shape
    T = B * S
    E = w_router.shape[1]
    assert S % tt == 0 and tt % LANES == 0 and E % (N_EXPERT_GROUPS * SUBLANES) == 0
    spt = S // tt
    nb = tt // LANES
    kern = functools.partial(_route_kernel, tt=tt, D=D, E=E)
    return pl.pallas_call(
        kern,
        grid=(T // tt,),
        in_specs=[pl.BlockSpec((tt, D), lambda i: (i, 0)),
                  pl.BlockSpec((1, 2, D), lambda i: (i // spt, 0, 0)),
                  pl.BlockSpec((1, D), lambda i: (0, 0)),
                  pl.BlockSpec((E, D), lambda i: (0, 0)),
                  pl.BlockSpec((E, 1), lambda i: (0, 0))],
        out_specs=[pl.BlockSpec((tt, D // 2), lambda i: (i, 0)),
                   pl.BlockSpec((nb, TOP_K, LANES), lambda i: (i, 0, 0)),
                   pl.BlockSpec((nb, TOP_K, LANES), lambda i: (i, 0, 0)),
                   pl.BlockSpec((tt, TOP_K), lambda i: (i, 0)),
                   pl.BlockSpec((E, LANES), lambda i: (0, 0))],
        out_shape=[jax.ShapeDtypeStruct((T, D // 2), i32),
                   jax.ShapeDtypeStruct((T // LANES, TOP_K, LANES), i32),
                   jax.ShapeDtypeStruct((T // LANES, TOP_K, LANES), i32),
                   jax.ShapeDtypeStruct((T, TOP_K), f32),
                   jax.ShapeDtypeStruct((E, LANES), f32)],
        scratch_shapes=[pltpu.VMEM((E, LANES), f32)],
        compiler_params=pltpu.CompilerParams(dimension_semantics=("arbitrary",), vmem_limit_bytes=VMEM_LIMIT),
        name="route",
    )(x1.reshape(T, D), mod_f, g_pre.reshape(1, D), w_router.T.astype(bf16), router_bias.reshape(E, 1))


def _plan_kernel(cnt_ref, eidx_ref, rank_ref, dest_ref, tstart_ref, total_ref, pstart_ref, *, E, TM, nb):
    i = pl.program_id(0)

    @pl.when(i == 0)
    def _():
        cnt_col = cnt_ref[:, 0:1]
        pad_col = jnp.ceil(cnt_col / TM) * TM
        r = lax.broadcasted_iota(i32, (E, E), 0)
        c = lax.broadcasted_iota(i32, (E, E), 1)
        pad_row = jnp.sum(jnp.where(r == c, pad_col, 0.0), axis=0, keepdims=True)
        pstart = jnp.sum(jnp.where(c < r, pad_row, 0.0), axis=1, keepdims=True)
        pstart_ref[...] = jnp.broadcast_to(pstart, (E, LANES))
        pstart_row = jnp.sum(jnp.where(r == c, pstart, 0.0), axis=0, keepdims=True)
        tstart_ref[...] = (pstart_row / TM).astype(i32)
        total = jnp.sum(pad_col, axis=0, keepdims=True)
        total_ref[...] = jnp.broadcast_to(total / TM, (1, LANES)).astype(i32)

    ei = lax.broadcasted_iota(i32, (E, LANES), 0)
    pstart = pstart_ref[...]
    for j in range(nb):
        for k in range(TOP_K):
            e_row = eidx_ref[j, k:k + 1, :]
            base = jnp.sum(jnp.where(ei == e_row, pstart, 0.0), axis=0, keepdims=True).astype(i32)
            dest_ref[j, k:k + 1, :] = base + rank_ref[j, k:k + 1, :]


def _plan(cnt, eidx, rank, *, TM):
    E = cnt.shape[0]
    NB = eidx.shape[0]
    nb = min(16, NB)
    assert NB % nb == 0
    kern = functools.partial(_plan_kernel, E=E, TM=TM, nb=nb)
    blk = pl.BlockSpec((nb, TOP_K, LANES), lambda i: (i, 0, 0))
    dest, tstart, total = pl.pallas_call(
        kern,
        grid=(NB // nb,),
        in_specs=[pl.BlockSpec((E, LANES), lambda i: (0, 0)), blk, blk],
        out_specs=[blk,
                   pl.BlockSpec((1, E), lambda i: (0, 0)),
                   pl.BlockSpec((1, LANES), lambda i: (0, 0))],
        out_shape=[jax.ShapeDtypeStruct((NB, TOP_K, LANES), i32),
                   jax.ShapeDtypeStruct((1, E), i32),
                   jax.ShapeDtypeStruct((1, LANES), i32)],
        scratch_shapes=[pltpu.VMEM((E, LANES), f32)],
        compiler_params=pltpu.CompilerParams(dimension_semantics=("arbitrary",)),
        name="plan",
    )(cnt, eidx, rank)
    return dest, jnp.concatenate([tstart.reshape(E), total.reshape(LANES)[0:1]])


def _sc_workers():
    info = plsc.get_sparse_core_info()
    return info.num_cores, info.num_subcores


def _dispatch(h2p, dest, NP):
    T, W = h2p.shape
    NC, NS = _sc_workers()
    nblk = T // (NC * NS * SC_ROWS)
    assert T == nblk * NC * NS * SC_ROWS
    mesh = plsc.VectorSubcoreMesh(core_axis_name="c", subcore_axis_name="s")

    @functools.partial(
        pl.kernel, mesh=mesh, out_type=jax.ShapeDtypeStruct((NP, W), h2p.dtype),
        scratch_types=[pltpu.VMEM((TOP_K, SC_ROWS), i32), pltpu.VMEM((SC_ROWS, W), h2p.dtype)],
        name="dispatch")
    def kern(h_hbm, dest_hbm, xs_hbm, idx_v, rows_v):
        wid = lax.axis_index("s") * NC + lax.axis_index("c")

        @pl.loop(0, nblk)
        def _(b):
            blk = wid * nblk + b
            pltpu.sync_copy(dest_hbm.at[blk], idx_v)
            pltpu.sync_copy(h_hbm.at[pl.ds(pl.multiple_of(blk * SC_ROWS, SC_ROWS), SC_ROWS)], rows_v)
            for k in range(TOP_K):
                pltpu.sync_copy(rows_v, xs_hbm.at[idx_v.at[k]])

    return kern(h2p, dest)


def _collect(ys, dest, t0, tc):
    NP, W = ys.shape
    NC, NS = _sc_workers()
    nblk = tc // (NC * NS * SC_ROWS)
    assert tc == nblk * NC * NS * SC_ROWS and t0 % SC_ROWS == 0
    blk0 = t0 // SC_ROWS
    mesh = plsc.VectorSubcoreMesh(core_axis_name="c", subcore_axis_name="s")

    @functools.partial(
        pl.kernel, mesh=mesh, out_type=jax.ShapeDtypeStruct((TOP_K, tc, W), ys.dtype),
        scratch_types=[pltpu.VMEM((TOP_K, SC_ROWS), i32), pltpu.VMEM((SC_ROWS, W), ys.dtype)],
        name="collect")
    def kern(ys_hbm, dest_hbm, yg_hbm, idx_v, rows_v):
        wid = lax.axis_index("s") * NC + lax.axis_index("c")

        @pl.loop(0, nblk)
        def _(b):
            blk = wid * nblk + b
            pltpu.sync_copy(dest_hbm.at[blk0 + blk], idx_v)
            for k in range(TOP_K):
                pltpu.sync_copy(ys_hbm.at[idx_v.at[k]], rows_v)
                pltpu.sync_copy(rows_v, yg_hbm.at[k, pl.ds(pl.multiple_of(blk * SC_ROWS, SC_ROWS), SC_ROWS)])

    return kern(ys, dest)


def _experts_kernel(tstart_ref, xs_hbm, wgu_ref, wd_ref, ys_hbm, xbuf, ybuf, actbuf, state, sem_in, sem_out,
                    wgu_bf, wd_bf, *, TM, ED, D):
    e = pl.program_id(0)
    n_exp = pl.num_programs(0)
    u_lo, u_hi, n_tiles = tstart_ref[e], tstart_ref[e + 1], tstart_ref[n_exp]
    sub = TM // N_SUB

    def in_copy(u, slot):
        rows = pl.ds(pl.multiple_of(u * TM, TM), TM)
        return pltpu.make_async_copy(xs_hbm.at[rows], xbuf.at[slot], sem_in.at[slot])

    def out_copy(u, slot):
        rows = pl.ds(pl.multiple_of(u * TM, TM), TM)
        return pltpu.make_async_copy(ybuf.at[slot], ys_hbm.at[rows], sem_out.at[slot])

    def wait_in(u):
        in_copy(u, lax.rem(u, N_IN)).wait()

    def start_in_ahead(u):
        ahead = u + (N_IN - 1)

        @pl.when(ahead < n_tiles)
        def _():
            in_copy(ahead, lax.rem(ahead, N_IN)).start()

    def wait_out_slot(u):
        @pl.when(u >= N_OUT)
        def _():
            out_copy(u - N_OUT, lax.rem(u, N_OUT)).wait()

    def start_out(u):
        out_copy(u, lax.rem(u, N_OUT)).start()

    def up(u):
        slot, aslot = lax.rem(u, N_IN), lax.rem(u, 2)
        for j in range(N_SUB):
            rows = slice(j * sub, (j + 1) * sub)
            lo, hi = _unpack_bf16_pair(xbuf[slot, rows, :])
            gu = (jnp.dot(lo.astype(bf16), wgu_bf[0:D // 2, :], preferred_element_type=f32)
                  + jnp.dot(hi.astype(bf16), wgu_bf[D // 2:, :], preferred_element_type=f32))
            actbuf[aslot, rows, :] = (_silu(gu[:, :ED]) * gu[:, ED:]).astype(bf16)

    def down(u, wslot):
        oslot, aslot = lax.rem(u, N_OUT), lax.rem(u, 2)
        for j in range(N_SUB):
            rows = slice(j * sub, (j + 1) * sub)
            y = jnp.dot(actbuf[aslot, rows, :], wd_bf[wslot], preferred_element_type=f32)
            ybuf[oslot, rows, :] = _pack_bf16_pair(y[:, :D // 2], y[:, D // 2:])

    @pl.when(e == 0)
    def _():
        state[0] = 0
        state[1] = 0
        for u in range(N_IN - 1):
            @pl.when(u < n_tiles)
            def _():
                in_copy(u, u).start()

    @pl.when(u_hi > u_lo)
    def _():
        old_w = state[1]
        new_w = 1 - old_w

        @pl.when(state[0] == 1)
        def _():
            wait_in(u_lo)
            wait_out_slot(u_lo - 1)
            down(u_lo - 1, old_w)
            wgu_bf[...] = wgu_ref[0].astype(bf16)
            wd_bf[new_w] = wd_ref[0].astype(bf16)
            up(u_lo)
            start_out(u_lo - 1)
            start_in_ahead(u_lo)

        @pl.when(state[0] == 0)
        def _():
            wait_in(u_lo)
            wgu_bf[...] = wgu_ref[0].astype(bf16)
            wd_bf[new_w] = wd_ref[0].astype(bf16)
            up(u_lo)
            start_in_ahead(u_lo)

        def skewed(u, carry):
            wait_in(u)
            wait_out_slot(u - 1)
            down(u - 1, new_w)
            up(u)
            start_out(u - 1)
            start_in_ahead(u)
            return carry

        n_rest = u_hi - (u_lo + 1)

        def skewed_pair(i, carry):
            u = u_lo + 1 + 2 * i
            wait_in(u)
            wait_in(u + 1)
            wait_out_slot(u - 1)
            wait_out_slot(u)
            start_in_ahead(u)
            down(u - 1, new_w)
            up(u)
            down(u, new_w)
            up(u + 1)
            start_out(u - 1)
            start_out(u)
            start_in_ahead(u + 1)
            return carry

        lax.fori_loop(0, lax.shift_right_logical(n_rest, 1), skewed_pair, 0)

        @pl.when(lax.rem(n_rest, 2) == 1)
        def _():
            skewed(u_hi - 1, 0)

        state[0] = 1
        state[1] = new_w

    @pl.when(e == n_exp - 1)
    def _():
        @pl.when(state[0] == 1)
        def _():
            wait_out_slot(n_tiles - 1)
            down(n_tiles - 1, state[1])
            start_out(n_tiles - 1)

        for back in range(N_OUT, 0, -1):
            @pl.when(n_tiles >= back)
            def _():
                out_copy(n_tiles - back, lax.rem(n_tiles - back, N_OUT)).wait()


def _experts(xs, tstart, w_gate_up, w_down, *, TM):
    NP, W = xs.shape
    E, D, ED2 = w_gate_up.shape
    ED = ED2 // 2
    kern = functools.partial(_experts_kernel, TM=TM, ED=ED, D=D)
    return pl.pallas_call(
        kern,
        grid_spec=pltpu.PrefetchScalarGridSpec(
            num_scalar_prefetch=1,
            grid=(E,),
            in_specs=[pl.BlockSpec(memory_space=pl.ANY),
                      pl.BlockSpec((1, D, ED2), lambda e, ts: (e, 0, 0)),
                      pl.BlockSpec((1, ED, D), lambda e, ts: (e, 0, 0))],
            out_specs=pl.BlockSpec(memory_space=pl.ANY),
            scratch_shapes=[pltpu.VMEM((N_IN, TM, W), i32), pltpu.VMEM((N_OUT, TM, W), i32),
                            pltpu.VMEM((2, TM, ED), bf16), pltpu.SMEM((2,), i32),
                            pltpu.SemaphoreType.DMA((N_IN,)), pltpu.SemaphoreType.DMA((N_OUT,)),
                            pltpu.VMEM((D, ED2), bf16), pltpu.VMEM((2, ED, D), bf16)]),
        out_shape=jax.ShapeDtypeStruct((NP, W), i32),
        compiler_params=pltpu.CompilerParams(dimension_semantics=("arbitrary",), vmem_limit_bytes=VMEM_LIMIT),
        name="experts",
    )(tstart, xs, w_gate_up, w_down)


def _final_kernel(yg_ref, wcol_ref, h2p_ref, x1_ref, mod_ref, gpost_ref, wsgu_ref, wsd_ref, o_ref, *, D, SD):
    wcol = wcol_ref[...]
    r_lo = jnp.zeros(h2p_ref.shape, f32)
    r_hi = jnp.zeros(h2p_ref.shape, f32)
    for k in range(TOP_K):
        lo, hi = _unpack_bf16_pair(yg_ref[k])
        wk = wcol[:, k:k + 1]
        r_lo = r_lo + wk * lo
        r_hi = r_hi + wk * hi
    routed = jnp.concatenate([r_lo, r_hi], axis=-1)

    hlo, hhi = _unpack_bf16_pair(h2p_ref[...])
    gs = (jnp.dot(hlo.astype(bf16), wsgu_ref[0:D // 2, :], preferred_element_type=f32)
          + jnp.dot(hhi.astype(bf16), wsgu_ref[D // 2:, :], preferred_element_type=f32))
    sh = (_silu(gs[:, :SD]) * gs[:, SD:]).astype(bf16)
    y = routed + jnp.dot(sh, wsd_ref[...], preferred_element_type=f32)
    yn = y * lax.rsqrt(jnp.mean(y * y, axis=-1, keepdims=True) + EPS) * gpost_ref[...]
    o_ref[...] = x1_ref[...] + mod_ref[0, 2:3, :] * yn


def _final_kernel_chained(yg_ref, wcol_ref, h2p_ref, x1_ref, mod_ref, gpost_ref, wsgu_ref, wsd_ref, prev_ref, o_ref,
                          *, D, SD):
    del prev_ref
    _final_kernel(yg_ref, wcol_ref, h2p_ref, x1_ref, mod_ref, gpost_ref, wsgu_ref, wsd_ref, o_ref, D=D, SD=SD)


def _final(yg, wcol, h2p, x1, mod_f, g_post, wsgu, wsd, out_prev, *, t0, tt, S):
    T, D = x1.shape
    tc = yg.shape[1]
    SD = wsd.shape[0]
    spt = S // tt
    i0 = t0 // tt
    assert t0 % tt == 0 and tc % tt == 0
    kern = functools.partial(_final_kernel, D=D, SD=SD)
    in_specs = [pl.BlockSpec((TOP_K, tt, D // 2), lambda i: (0, i, 0)),
                pl.BlockSpec((tt, TOP_K), lambda i: (i0 + i, 0)),
                pl.BlockSpec((tt, D // 2), lambda i: (i0 + i, 0)),
                pl.BlockSpec((tt, D), lambda i: (i0 + i, 0)),
                pl.BlockSpec((1, 3, D), lambda i: ((i0 + i) // spt, 0, 0)),
                pl.BlockSpec((1, D), lambda i: (0, 0)),
                pl.BlockSpec((D, 2 * SD), lambda i: (0, 0)),
                pl.BlockSpec((SD, D), lambda i: (0, 0))]
    args = [yg, wcol, h2p, x1, mod_f, g_post.reshape(1, D), wsgu, wsd]
    aliases = {}
    if out_prev is not None:
        in_specs.append(pl.BlockSpec(memory_space=pl.ANY))
        args.append(out_prev)
        aliases = {len(args) - 1: 0}
        kern = functools.partial(_final_kernel_chained, D=D, SD=SD)
    return pl.pallas_call(
        kern,
        grid=(tc // tt,),
        in_specs=in_specs,
        out_specs=pl.BlockSpec((tt, D), lambda i: (i0 + i, 0)),
        out_shape=jax.ShapeDtypeStruct((T, D), f32),
        input_output_aliases=aliases,
        compiler_params=pltpu.CompilerParams(dimension_semantics=("arbitrary",), vmem_limit_bytes=VMEM_LIMIT),
        name="final",
    )(*args)


def _layer(x, mod, g_pre_mix, g_post_mix, g_pre_ffn, g_post_ffn, w_in, w_dw, b_dw, ln_g, ln_b, w_conv_out,
           w_pool, b_pool, pool_scale, w_out, w_router, router_bias, w_gate_up, w_down, ws_gate_up, ws_down):
    B, S, D = x.shape
    T = B * S
    E = w_router.shape[1]
    mod6 = mod.reshape(B, 6, D)
    mod_m, mod_f = mod6[:, 0:3], mod6[:, 3:6]
    x1 = _mixer(x, mod_m, g_pre_mix, g_post_mix, w_in, w_dw, b_dw, ln_g, ln_b, w_conv_out, w_pool, b_pool,
                pool_scale, w_out, ts=min(512, S))
    h2p, eidx, rank, wcol, cnt = _route(x1, mod_f[:, 0:2], g_pre_ffn, w_router, router_bias, tt=min(512, S))
    TM = 512
    NT = (T * TOP_K) // TM + E
    dest, tstart = _plan(cnt, eidx, rank, TM=TM)
    xs = _dispatch(h2p, dest, NT * TM)
    ys = _experts(xs, tstart, w_gate_up, w_down, TM=TM)
    x1f = x1.reshape(T, D)
    wsgu, wsd = ws_gate_up.astype(bf16), ws_down.astype(bf16)
    tt = min(512, S)
    n_chunks = N_TAIL_CHUNKS
    while (T // n_chunks) % tt:
        n_chunks //= 2
    tc = T // n_chunks
    out = None
    for ci in range(n_chunks):
        yg = _collect(ys, dest, ci * tc, tc)
        out = _final(yg, wcol, h2p, x1f, mod_f, g_post_ffn, wsgu, wsd, out, t0=ci * tc, tt=tt, S=S)
    return out.reshape(B, S, D)


def kernel(x, c, w_ada, b_ada, g_pre_mix, g_post_mix, g_pre_ffn, g_post_ffn, w_in, w_dw, b_dw, ln_g, ln_b,
           w_conv_out, w_pool, b_pool, pool_scale, w_out, w_router, router_bias, w_gate_up, w_down, ws_gate_up,
           ws_down):
    depth = w_ada.shape[0]
    for l in range(depth):
        mod = _adaln(c, w_ada[l], b_ada[l])
        x = _layer(x, mod, g_pre_mix[l], g_post_mix[l], g_pre_ffn[l], g_post_ffn[l], w_in[l], w_dw[l], b_dw[l],
                   ln_g[l], ln_b[l], w_conv_out[l], w_pool[l], b_pool[l], pool_scale[l], w_out[l], w_router[l],
                   router_bias[l], w_gate_up[l], w_down[l], ws_gate_up[l], ws_down[l])
    return x
```

```python
import functools

import jax
import jax.numpy as jnp
from jax import lax
from jax.experimental import pallas as pl
from jax.experimental.pallas import tpu as pltpu
from jax.experimental.pallas import tpu_sc as plsc

EPS = 1e-6
TOP_K = 8
N_EXPERT_GROUPS = 8
TOPK_GROUPS = 4
ROUTED_SCALE = 2.5
POOL_WINDOWS = (2, 4, 8, 16)

LANES = 128
SUBLANES = 8
CONV_HIST = 32
POOL_HIST = 16
CONV_ROWS = 32
N_IN = 6
N_OUT = 4
N_SUB = 2
N_TAIL_CHUNKS = 8
SC_ROWS = 128
VMEM_LIMIT = 56 * 1024 * 1024

f32 = jnp.float32
bf16 = jnp.bfloat16
i32 = jnp.int32


def _sigmoid(v):
    return 0.5 * jnp.tanh(0.5 * v) + 0.5


def _sigmoid_of_half(hv):
    return 0.5 * jnp.tanh(hv) + 0.5


def _silu(v):
    return v * _sigmoid(v)


def _pack_bf16_pair(lo, hi):
    lo_bits = lax.bitcast_convert_type(lo.astype(bf16).astype(f32), jnp.uint32)
    hi_bits = lax.bitcast_convert_type(hi.astype(bf16).astype(f32), jnp.uint32)
    word = lax.shift_right_logical(lo_bits, jnp.uint32(16)) | hi_bits
    return lax.bitcast_convert_type(word, i32)


def _unpack_bf16_pair(word):
    w = lax.bitcast_convert_type(word, jnp.uint32)
    lo = lax.bitcast_convert_type(lax.shift_left(w, jnp.uint32(16)), f32)
    hi = lax.bitcast_convert_type(w & jnp.uint32(0xFFFF0000), f32)
    return lo, hi


def _adaln_kernel(c_ref, w_ref, b_ref, o_ref):
    cond = _silu(c_ref[...])
    o_ref[...] = jnp.dot(cond.astype(bf16), w_ref[...].astype(bf16), preferred_element_type=f32) + b_ref[...]


def _adaln(c, w_ada, b_ada):
    B, D = c.shape
    N = w_ada.shape[1]
    tn = 1024
    return pl.pallas_call(
        _adaln_kernel,
        grid=(N // tn,),
        in_specs=[pl.BlockSpec((B, D), lambda j: (0, 0)),
                  pl.BlockSpec((D, tn), lambda j: (0, j)),
                  pl.BlockSpec((1, tn), lambda j: (0, j))],
        out_specs=pl.BlockSpec((B, tn), lambda j: (0, j)),
        out_shape=jax.ShapeDtypeStruct((B, N), f32),
        name="adaln",
    )(c, w_ada, b_ada.reshape(1, N))


def _mixer_kernel(x_ref, mod_ref, gpre_ref, gpost_ref, win_ref, wdw_ref, bdw_ref, lng_ref, lnb_ref,
                  wco_ref, wpool_ref, bpool_ref, pscale_ref, wout_ref, o_ref,
                  aext_ref, ashift_ref, pext_ref, act_ref, *, ts, D, conv_k):
    s = pl.program_id(1)

    @pl.when(s == 0)
    def _():
        aext_ref[0:CONV_HIST, :] = jnp.zeros((CONV_HIST, D), f32)
        pext_ref[0:POOL_HIST, :] = jnp.zeros((POOL_HIST, D), f32)

    x = x_ref[0]
    shift, scale, gate = mod_ref[0, 0:1, :], mod_ref[0, 1:2, :], mod_ref[0, 2:3, :]
    hb = (x * lax.rsqrt(jnp.mean(x * x, axis=-1, keepdims=True) + EPS) * (gpre_ref[...] * (1.0 + scale))
          + shift).astype(bf16)

    a_val = jnp.dot(hb, win_ref[:, 0:D], preferred_element_type=f32)
    a_gate = jnp.dot(hb, win_ref[:, D:2 * D], preferred_element_type=f32)
    aext_ref[CONV_HIST:CONV_HIST + ts, :] = a_val * _sigmoid_of_half(a_gate)
    n_ext = ts + CONV_HIST
    a_full = aext_ref[...]
    for r in range(1, SUBLANES):
        ashift_ref[r - 1] = pltpu.roll(a_full, n_ext - r, axis=0)
    rc = CONV_ROWS
    off0 = CONV_HIST - (conv_k - 1)
    for c in range(ts // rc):
        acc = jnp.broadcast_to(bdw_ref[...], (rc, D))
        for k in range(conv_k):
            r, q = (off0 + k) % SUBLANES, (off0 + k) // SUBLANES
            row0 = c * rc + q * SUBLANES
            tap = aext_ref[row0:row0 + rc, :] if r == 0 else ashift_ref[r - 1, row0:row0 + rc, :]
            acc = acc + jnp.concatenate([wdw_ref[k]] * (rc // SUBLANES), axis=0) * tap
        mu = jnp.mean(acc, axis=-1, keepdims=True)
        cen = acc - mu
        var = jnp.mean(cen * cen, axis=-1, keepdims=True)
        ln = cen * lax.rsqrt(var + EPS) * lng_ref[...] + lnb_ref[...]
        act_ref[c * rc:(c + 1) * rc, :] = _silu(ln).astype(bf16)
    aext_ref[0:CONV_HIST, :] = aext_ref[ts:ts + CONV_HIST, :]
    a_out = jnp.dot(act_ref[...], wco_ref[...], preferred_element_type=f32)

    pext_ref[POOL_HIST:POOL_HIST + ts, :] = jnp.dot(hb, win_ref[:, 2 * D:3 * D], preferred_element_type=f32)
    t_glob = s * ts + lax.broadcasted_iota(i32, (ts, 1), 0)
    dg = D // len(POOL_WINDOWS)
    bms = []
    for g, win in enumerate(POOL_WINDOWS):
        cols = slice(g * dg, (g + 1) * dg)
        p_ext = pext_ref[:, cols]
        wsum, j = p_ext, 1
        while j < win:
            wsum = wsum + pltpu.roll(wsum, j, axis=0)
            j *= 2
        cur = p_ext[POOL_HIST:, :]
        cnt = jnp.minimum(t_glob + 1, win).astype(f32)
        pooled = wsum[POOL_HIST:, :] / cnt - cur
        bms.append(jnp.dot(pooled.astype(bf16), wpool_ref[g], preferred_element_type=f32))
    pext_ref[0:POOL_HIST, :] = pext_ref[ts:ts + POOL_HIST, :]
    bm = (jnp.concatenate(bms, axis=-1) + bpool_ref[...]) * pscale_ref[...]

    g_a = _sigmoid_of_half(jnp.dot(hb, win_ref[:, 3 * D:4 * D], preferred_element_type=f32))
    g_b = _sigmoid_of_half(jnp.dot(hb, win_ref[:, 4 * D:5 * D], preferred_element_type=f32))
    mixed = (g_a * a_out + g_b * bm).astype(bf16)
    y = jnp.dot(mixed, wout_ref[...], preferred_element_type=f32)
    o_ref[0] = x + y * lax.rsqrt(jnp.mean(y * y, axis=-1, keepdims=True) + EPS) * (gpost_ref[...] * gate)


def _mixer(x, mod_m, g_pre, g_post, w_in, w_dw, b_dw, ln_g, ln_b, w_conv_out, w_pool, b_pool, pool_scale,
           w_out, *, ts):
    B, S, D = x.shape
    conv_k = w_dw.shape[0]
    assert conv_k - 1 <= CONV_HIST and max(POOL_WINDOWS) - 1 <= POOL_HIST
    assert S % ts == 0 and ts >= CONV_HIST and D % (LANES * len(POOL_WINDOWS)) == 0
    row = lambda v: v.reshape(1, D)
    vm = pl.BlockSpec(memory_space=pltpu.VMEM)
    half_cols = jnp.concatenate([jnp.ones((D,), f32), jnp.full((D,), 0.5, f32), jnp.ones((D,), f32),
                                 jnp.full((2 * D,), 0.5, f32)])[None, :]
    kern = functools.partial(_mixer_kernel, ts=ts, D=D, conv_k=conv_k)
    return pl.pallas_call(
        kern,
        grid=(B, S // ts),
        in_specs=[pl.BlockSpec((1, ts, D), lambda b, s: (b, s, 0)),
                  pl.BlockSpec((1, 3, D), lambda b, s: (b, 0, 0)),
                  vm, vm, vm, vm, vm, vm, vm, vm, vm, vm, vm, vm],
        out_specs=pl.BlockSpec((1, ts, D), lambda b, s: (b, s, 0)),
        out_shape=jax.ShapeDtypeStruct((B, S, D), f32),
        scratch_shapes=[pltpu.VMEM((ts + CONV_HIST, D), f32),
                        pltpu.VMEM((SUBLANES - 1, ts + CONV_HIST, D), f32),
                        pltpu.VMEM((ts + POOL_HIST, D), f32),
                        pltpu.VMEM((ts, D), bf16)],
        compiler_params=pltpu.CompilerParams(dimension_semantics=("arbitrary", "arbitrary"),
                                             vmem_limit_bytes=VMEM_LIMIT),
        name="mixer",
    )(x, mod_m, row(g_pre), row(g_post), (w_in * half_cols).astype(bf16),
      jnp.broadcast_to(w_dw[:, None, :], (conv_k, SUBLANES, D)), row(b_dw), row(ln_g), row(ln_b),
      w_conv_out.astype(bf16), w_pool.astype(bf16), row(b_pool), row(pool_scale), w_out.astype(bf16))


def _route_kernel(x_ref, mod_ref, gpre_ref, wr_ref, bias_ref,
                  h2p_ref, eidx_ref, rank_ref, wcol_ref, cnt_ref, carry_ref, *, tt, D, E):
    i = pl.program_id(0)

    @pl.when(i == 0)
    def _():
        carry_ref[...] = jnp.zeros((E, LANES), f32)

    x = x_ref[...]
    shift, scale = mod_ref[0, 0:1, :], mod_ref[0, 1:2, :]
    h = x * lax.rsqrt(jnp.mean(x * x, axis=-1, keepdims=True) + EPS) * gpre_ref[...]
    h = h * (1.0 + scale) + shift
    h2p_ref[...] = _pack_bf16_pair(h[:, :D // 2], h[:, D // 2:])

    logits = lax.dot_general(wr_ref[...], h.astype(bf16), (((1,), (1,)), ((), ())),
                             preferred_element_type=f32)
    scores = _sigmoid(logits)
    choice = scores + bias_ref[...]
    G = N_EXPERT_GROUPS
    ge = E // G
    neg = jnp.float32(-jnp.inf)

    ch3 = choice.reshape(G, ge, tt)
    m1 = jnp.max(ch3, axis=1, keepdims=True)
    is_m1 = ch3 == m1
    n_m1 = jnp.sum(is_m1.astype(f32), axis=1, keepdims=True)
    m2 = jnp.max(jnp.where(is_m1, neg, ch3), axis=1, keepdims=True)
    gs = (m1 + jnp.where(n_m1 >= 2.0, m1, m2)).reshape(G, tt)

    gi = lax.broadcasted_iota(i32, (G, tt), 0).astype(f32)
    gsel = jnp.zeros((G, tt), f32)
    for _ in range(TOPK_GROUPS):
        gm = jnp.max(gs, axis=0, keepdims=True)
        first = jnp.min(jnp.where(gs == gm, gi, float(G)), axis=0, keepdims=True)
        hit = gi == first
        gsel = jnp.where(hit, 1.0, gsel)
        gs = jnp.where(hit, neg, gs)

    emask = jnp.broadcast_to(gsel.reshape(G, 1, tt), (G, ge, tt)).reshape(E, tt)
    masked = jnp.where(emask > 0.5, choice, neg)
    ei = lax.broadcasted_iota(i32, (E, tt), 0).astype(f32)
    sel = jnp.zeros((E, tt), f32)
    idx_rows, sc_rows = [], []
    for _ in range(TOP_K):
        mx = jnp.max(masked, axis=0, keepdims=True)
        first = jnp.min(jnp.where(masked == mx, ei, float(E)), axis=0, keepdims=True)
        hit = ei == first
        idx_rows.append(first)
        sc_rows.append(jnp.sum(jnp.where(hit, scores, 0.0), axis=0, keepdims=True))
        sel = sel + hit.astype(f32)
        masked = jnp.where(hit, neg, masked)

    upper = (lax.broadcasted_iota(i32, (tt, tt), 0) < lax.broadcasted_iota(i32, (tt, tt), 1)).astype(bf16)
    selb = sel.astype(bf16)
    excl = jnp.dot(selb, upper, preferred_element_type=f32)
    carry = carry_ref[...]
    pos = excl + jnp.concatenate([carry] * (tt // LANES), axis=1)
    carry = carry + jnp.dot(selb, jnp.ones((tt, LANES), bf16), preferred_element_type=f32)
    carry_ref[...] = carry
    cnt_ref[...] = carry

    ssum = sc_rows[0]
    for r in sc_rows[1:]:
        ssum = ssum + r
    wscale = ROUTED_SCALE / ssum
    nb = tt // LANES
    for k in range(TOP_K):
        rk = jnp.sum(jnp.where(ei == idx_rows[k], pos, 0.0), axis=0, keepdims=True).astype(i32)
        for j in range(nb):
            eidx_ref[j, k:k + 1, :] = idx_rows[k][:, j * LANES:(j + 1) * LANES].astype(i32)
            rank_ref[j, k:k + 1, :] = rk[:, j * LANES:(j + 1) * LANES]
    wstack = jnp.concatenate([r * wscale for r in sc_rows] + [jnp.zeros((LANES - TOP_K, tt), f32)], axis=0)
    for j in range(nb):
        blk = wstack[:, j * LANES:(j + 1) * LANES].T
        wcol_ref[j * LANES:(j + 1) * LANES, :] = blk[:, 0:TOP_K]


def _route(x1, mod_f, g_pre, w_router, router_bias, *, tt):
    B, S, D = x1.shape
    T = B * S
    E = w_router.shape[1]
    assert S % tt == 0 and tt % LANES == 0 and E % (N_EXPERT_GROUPS * SUBLANES) == 0
    spt = S // tt
    nb = tt // LANES
    kern = functools.partial(_route_kernel, tt=tt, D=D, E=E)
    return pl.pallas_call(
        kern,
        grid=(T // tt,),
        in_specs=[pl.BlockSpec((tt, D), lambda i: (i, 0)),
                  pl.BlockSpec((1, 2, D), lambda i: (i // spt, 0, 0)),
                  pl.BlockSpec((1, D), lambda i: (0, 0)),
                  pl.BlockSpec((E, D), lambda i: (0, 0)),
                  pl.BlockSpec((E, 1), lambda i: (0, 0))],
        out_specs=[pl.BlockSpec((tt, D // 2), lambda i: (i, 0)),
                   pl.BlockSpec((nb, TOP_K, LANES), lambda i: (i, 0, 0)),
                   pl.BlockSpec((nb, TOP_K, LANES), lambda i: (i, 0, 0)),
                   pl.BlockSpec((tt, TOP_K), lambda i: (i, 0)),
                   pl.BlockSpec((E, LANES), lambda i: (0, 0))],
        out_shape=[jax.ShapeDtypeStruct((T, D // 2), i32),
                   jax.ShapeDtypeStruct((T // LANES, TOP_K, LANES), i32),
                   jax.ShapeDtypeStruct((T // LANES, TOP_K, LANES), i32),
                   jax.ShapeDtypeStruct((T, TOP_K), f32),
                   jax.ShapeDtypeStruct((E, LANES), f32)],
        scratch_shapes=[pltpu.VMEM((E, LANES), f32)],
        compiler_params=pltpu.CompilerParams(dimension_semantics=("arbitrary",), vmem_limit_bytes=VMEM_LIMIT),
        name="route",
    )(x1.reshape(T, D), mod_f, g_pre.reshape(1, D), w_router.T.astype(bf16), router_bias.reshape(E, 1))


def _plan_kernel(cnt_ref, eidx_ref, rank_ref, dest_ref, tstart_ref, total_ref, pstart_ref, *, E, TM, nb):
    i = pl.program_id(0)

    @pl.when(i == 0)
    def _():
        cnt_col = cnt_ref[:, 0:1]
        pad_col = jnp.ceil(cnt_col / TM) * TM
        r = lax.broadcasted_iota(i32, (E, E), 0)
        c = lax.broadcasted_iota(i32, (E, E), 1)
        pad_row = jnp.sum(jnp.where(r == c, pad_col, 0.0), axis=0, keepdims=True)
        pstart = jnp.sum(jnp.where(c < r, pad_row, 0.0), axis=1, keepdims=True)
        pstart_ref[...] = jnp.broadcast_to(pstart, (E, LANES))
        pstart_row = jnp.sum(jnp.where(r == c, pstart, 0.0), axis=0, keepdims=True)
        tstart_ref[...] = (pstart_row / TM).astype(i32)
        total = jnp.sum(pad_col, axis=0, keepdims=True)
        total_ref[...] = jnp.broadcast_to(total / TM, (1, LANES)).astype(i32)

    ei = lax.broadcasted_iota(i32, (E, LANES), 0)
    pstart = pstart_ref[...]
    for j in range(nb):
        for k in range(TOP_K):
            e_row = eidx_ref[j, k:k + 1, :]
            base = jnp.sum(jnp.where(ei == e_row, pstart, 0.0), axis=0, keepdims=True).astype(i32)
            dest_ref[j, k:k + 1, :] = base + rank_ref[j, k:k + 1, :]


def _plan(cnt, eidx, rank, *, TM):
    E = cnt.shape[0]
    NB = eidx.shape[0]
    nb = min(16, NB)
    assert NB % nb == 0
    kern = functools.partial(_plan_kernel, E=E, TM=TM, nb=nb)
    blk = pl.BlockSpec((nb, TOP_K, LANES), lambda i: (i, 0, 0))
    dest, tstart, total = pl.pallas_call(
        kern,
        grid=(NB // nb,),
        in_specs=[pl.BlockSpec((E, LANES), lambda i: (0, 0)), blk, blk],
        out_specs=[blk,
                   pl.BlockSpec((1, E), lambda i: (0, 0)),
                   pl.BlockSpec((1, LANES), lambda i: (0, 0))],
        out_shape=[jax.ShapeDtypeStruct((NB, TOP_K, LANES), i32),
                   jax.ShapeDtypeStruct((1, E), i32),
                   jax.ShapeDtypeStruct((1, LANES), i32)],
        scratch_shapes=[pltpu.VMEM((E, LANES), f32)],
        compiler_params=pltpu.CompilerParams(dimension_semantics=("arbitrary",)),
        name="plan",
    )(cnt, eidx, rank)
    return dest, jnp.concatenate([tstart.reshape(E), total.reshape(LANES)[0:1]])


def _sc_workers():
    info = plsc.get_sparse_core_info()
    return info.num_cores, info.num_subcores


def _dispatch(h2p, dest, NP):
    T, W = h2p.shape
    NC, NS = _sc_workers()
    nblk = T // (NC * NS * SC_ROWS)
    assert T == nblk * NC * NS * SC_ROWS
    mesh = plsc.VectorSubcoreMesh(core_axis_name="c", subcore_axis_name="s")

    @functools.partial(
        pl.kernel, mesh=mesh, out_type=jax.ShapeDtypeStruct((NP, W), h2p.dtype),
        scratch_types=[pltpu.VMEM((TOP_K, SC_ROWS), i32), pltpu.VMEM((SC_ROWS, W), h2p.dtype)],
        name="dispatch")
    def kern(h_hbm, dest_hbm, xs_hbm, idx_v, rows_v):
        wid = lax.axis_index("s") * NC + lax.axis_index("c")

        @pl.loop(0, nblk)
        def _(b):
            blk = wid * nblk + b
            pltpu.sync_copy(dest_hbm.at[blk], idx_v)
            pltpu.sync_copy(h_hbm.at[pl.ds(pl.multiple_of(blk * SC_ROWS, SC_ROWS), SC_ROWS)], rows_v)
            for k in range(TOP_K):
                pltpu.sync_copy(rows_v, xs_hbm.at[idx_v.at[k]])

    return kern(h2p, dest)


def _collect(ys, dest, t0, tc):
    NP, W = ys.shape
    NC, NS = _sc_workers()
    nblk = tc // (NC * NS * SC_ROWS)
    assert tc == nblk * NC * NS * SC_ROWS and t0 % SC_ROWS == 0
    blk0 = t0 // SC_ROWS
    mesh = plsc.VectorSubcoreMesh(core_axis_name="c", subcore_axis_name="s")

    @functools.partial(
        pl.kernel, mesh=mesh, out_type=jax.ShapeDtypeStruct((TOP_K, tc, W), ys.dtype),
        scratch_types=[pltpu.VMEM((TOP_K, SC_ROWS), i32), pltpu.VMEM((SC_ROWS, W), ys.dtype)],
        name="collect")
    def kern(ys_hbm, dest_hbm, yg_hbm, idx_v, rows_v):
        wid = lax.axis_index("s") * NC + lax.axis_index("c")

        @pl.loop(0, nblk)
        def _(b):
            blk = wid * nblk + b
            pltpu.sync_copy(dest_hbm.at[blk0 + blk], idx_v)
            for k in range(TOP_K):
                pltpu.sync_copy(ys_hbm.at[idx_v.at[k]], rows_v)
                pltpu.sync_copy(rows_v, yg_hbm.at[k, pl.ds(pl.multiple_of(blk * SC_ROWS, SC_ROWS), SC_ROWS)])

    return kern(ys, dest)


def _experts_kernel(tstart_ref, xs_hbm, wgu_ref, wd_ref, ys_hbm, xbuf, ybuf, actbuf, state, sem_in, sem_out,
                    wgu_bf, wd_bf, *, TM, ED, D):
    e = pl.program_id(0)
    n_exp = pl.num_programs(0)
    u_lo, u_hi, n_tiles = tstart_ref[e], tstart_ref[e + 1], tstart_ref[n_exp]
    sub = TM // N_SUB

    def in_copy(u, slot):
        rows = pl.ds(pl.multiple_of(u * TM, TM), TM)
        return pltpu.make_async_copy(xs_hbm.at[rows], xbuf.at[slot], sem_in.at[slot])

    def out_copy(u, slot):
        rows = pl.ds(pl.multiple_of(u * TM, TM), TM)
        return pltpu.make_async_copy(ybuf.at[slot], ys_hbm.at[rows], sem_out.at[slot])

    def wait_in(u):
        in_copy(u, lax.rem(u, N_IN)).wait()

    def start_in_ahead(u):
        ahead = u + (N_IN - 1)

        @pl.when(ahead < n_tiles)
        def _():
            in_copy(ahead, lax.rem(ahead, N_IN)).start()

    def wait_out_slot(u):
        @pl.when(u >= N_OUT)
        def _():
            out_copy(u - N_OUT, lax.rem(u, N_OUT)).wait()

    def start_out(u):
        out_copy(u, lax.rem(u, N_OUT)).start()

    def up(u):
        slot, aslot = lax.rem(u, N_IN), lax.rem(u, 2)
        for j in range(N_SUB):
            rows = slice(j * sub, (j + 1) * sub)
            lo, hi = _unpack_bf16_pair(xbuf[slot, rows, :])
            gu = (jnp.dot(lo.astype(bf16), wgu_bf[0:D // 2, :], preferred_element_type=f32)
                  + jnp.dot(hi.astype(bf16), wgu_bf[D // 2:, :], preferred_element_type=f32))
            actbuf[aslot, rows, :] = (_silu(gu[:, :ED]) * gu[:, ED:]).astype(bf16)

    def down(u, wslot):
        oslot, aslot = lax.rem(u, N_OUT), lax.rem(u, 2)
        for j in range(N_SUB):
            rows = slice(j * sub, (j + 1) * sub)
            y = jnp.dot(actbuf[aslot, rows, :], wd_bf[wslot], preferred_element_type=f32)
            ybuf[oslot, rows, :] = _pack_bf16_pair(y[:, :D // 2], y[:, D // 2:])

    @pl.when(e == 0)
    def _():
        state[0] = 0
        state[1] = 0
        for u in range(N_IN - 1):
            @pl.when(u < n_tiles)
            def _():
                in_copy(u, u).start()

    @pl.when(u_hi > u_lo)
    def _():
        old_w = state[1]
        new_w = 1 - old_w

        @pl.when(state[0] == 1)
        def _():
            wait_in(u_lo)
            wait_out_slot(u_lo - 1)
            down(u_lo - 1, old_w)
            wgu_bf[...] = wgu_ref[0].astype(bf16)
            wd_bf[new_w] = wd_ref[0].astype(bf16)
            up(u_lo)
            start_out(u_lo - 1)
            start_in_ahead(u_lo)

        @pl.when(state[0] == 0)
        def _():
            wait_in(u_lo)
            wgu_bf[...] = wgu_ref[0].astype(bf16)
            wd_bf[new_w] = wd_ref[0].astype(bf16)
            up(u_lo)
            start_in_ahead(u_lo)

        def skewed(u, carry):
            wait_in(u)
            wait_out_slot(u - 1)
            down(u - 1, new_w)
            up(u)
            start_out(u - 1)
            start_in_ahead(u)
            return carry

        n_rest = u_hi - (u_lo + 1)

        def skewed_pair(i, carry):
            u = u_lo + 1 + 2 * i
            wait_in(u)
            wait_in(u + 1)
            wait_out_slot(u - 1)
            wait_out_slot(u)
            start_in_ahead(u)
            down(u - 1, new_w)
            up(u)
            down(u, new_w)
            up(u + 1)
            start_out(u - 1)
            start_out(u)
            start_in_ahead(u + 1)
            return carry

        lax.fori_loop(0, lax.shift_right_logical(n_rest, 1), skewed_pair, 0)

        @pl.when(lax.rem(n_rest, 2) == 1)
        def _():
            skewed(u_hi - 1, 0)

        state[0] = 1
        state[1] = new_w

    @pl.when(e == n_exp - 1)
    def _():
        @pl.when(state[0] == 1)
        def _():
            wait_out_slot(n_tiles - 1)
            down(n_tiles - 1, state[1])
            start_out(n_tiles - 1)

        for back in range(N_OUT, 0, -1):
            @pl.when(n_tiles >= back)
            def _():
                out_copy(n_tiles - back, lax.rem(n_tiles - back, N_OUT)).wait()


def _experts(xs, tstart, w_gate_up, w_down, *, TM):
    NP, W = xs.shape
    E, D, ED2 = w_gate_up.shape
    ED = ED2 // 2
    kern = functools.partial(_experts_kernel, TM=TM, ED=ED, D=D)
    return pl.pallas_call(
        kern,
        grid_spec=pltpu.PrefetchScalarGridSpec(
            num_scalar_prefetch=1,
            grid=(E,),
            in_specs=[pl.BlockSpec(memory_space=pl.ANY),
                      pl.BlockSpec((1, D, ED2), lambda e, ts: (e, 0, 0)),
                      pl.BlockSpec((1, ED, D), lambda e, ts: (e, 0, 0))],
            out_specs=pl.BlockSpec(memory_space=pl.ANY),
            scratch_shapes=[pltpu.VMEM((N_IN, TM, W), i32), pltpu.VMEM((N_OUT, TM, W), i32),
                            pltpu.VMEM((2, TM, ED), bf16), pltpu.SMEM((2,), i32),
                            pltpu.SemaphoreType.DMA((N_IN,)), pltpu.SemaphoreType.DMA((N_OUT,)),
                            pltpu.VMEM((D, ED2), bf16), pltpu.VMEM((2, ED, D), bf16)]),
        out_shape=jax.ShapeDtypeStruct((NP, W), i32),
        compiler_params=pltpu.CompilerParams(dimension_semantics=("arbitrary",), vmem_limit_bytes=VMEM_LIMIT),
        name="experts",
    )(tstart, xs, w_gate_up, w_down)


def _final_kernel(yg_ref, wcol_ref, h2p_ref, x1_ref, mod_ref, gpost_ref, wsgu_ref, wsd_ref, o_ref, *, D, SD):
    wcol = wcol_ref[...]
    r_lo = jnp.zeros(h2p_ref.shape, f32)
    r_hi = jnp.zeros(h2p_ref.shape, f32)
    for k in range(TOP_K):
        lo, hi = _unpack_bf16_pair(yg_ref[k])
        wk = wcol[:, k:k + 1]
        r_lo = r_lo + wk * lo
        r_hi = r_hi + wk * hi
    routed = jnp.concatenate([r_lo, r_hi], axis=-1)

    hlo, hhi = _unpack_bf16_pair(h2p_ref[...])
    gs = (jnp.dot(hlo.astype(bf16), wsgu_ref[0:D // 2, :], preferred_element_type=f32)
          + jnp.dot(hhi.astype(bf16), wsgu_ref[D // 2:, :], preferred_element_type=f32))
    sh = (_silu(gs[:, :SD]) * gs[:, SD:]).astype(bf16)
    y = routed + jnp.dot(sh, wsd_ref[...], preferred_element_type=f32)
    yn = y * lax.rsqrt(jnp.mean(y * y, axis=-1, keepdims=True) + EPS) * gpost_ref[...]
    o_ref[...] = x1_ref[...] + mod_ref[0, 2:3, :] * yn


def _final_kernel_chained(yg_ref, wcol_ref, h2p_ref, x1_ref, mod_ref, gpost_ref, wsgu_ref, wsd_ref, prev_ref, o_ref,
                          *, D, SD):
    del prev_ref
    _final_kernel(yg_ref, wcol_ref, h2p_ref, x1_ref, mod_ref, gpost_ref, wsgu_ref, wsd_ref, o_ref, D=D, SD=SD)


def _final(yg, wcol, h2p, x1, mod_f, g_post, wsgu, wsd, out_prev, *, t0, tt, S):
    T, D = x1.shape
    tc = yg.shape[1]
    SD = wsd.shape[0]
    spt = S // tt
    i0 = t0 // tt
    assert t0 % tt == 0 and tc % tt == 0
    kern = functools.partial(_final_kernel, D=D, SD=SD)
    in_specs = [pl.BlockSpec((TOP_K, tt, D // 2), lambda i: (0, i, 0)),
                pl.BlockSpec((tt, TOP_K), lambda i: (i0 + i, 0)),
                pl.BlockSpec((tt, D // 2), lambda i: (i0 + i, 0)),
                pl.BlockSpec((tt, D), lambda i: (i0 + i, 0)),
                pl.BlockSpec((1, 3, D), lambda i: ((i0 + i) // spt, 0, 0)),
                pl.BlockSpec((1, D), lambda i: (0, 0)),
                pl.BlockSpec((D, 2 * SD), lambda i: (0, 0)),
                pl.BlockSpec((SD, D), lambda i: (0, 0))]
    args = [yg, wcol, h2p, x1, mod_f, g_post.reshape(1, D), wsgu, wsd]
    aliases = {}
    if out_prev is not None:
        in_specs.append(pl.BlockSpec(memory_space=pl.ANY))
        args.append(out_prev)
        aliases = {len(args) - 1: 0}
        kern = functools.partial(_final_kernel_chained, D=D, SD=SD)
    return pl.pallas_call(
        kern,
        grid=(tc // tt,),
        in_specs=in_specs,
        out_specs=pl.BlockSpec((tt, D), lambda i: (i0 + i, 0)),
        out_shape=jax.ShapeDtypeStruct((T, D), f32),
        input_output_aliases=aliases,
        compiler_params=pltpu.CompilerParams(dimension_semantics=("arbitrary",), vmem_limit_bytes=VMEM_LIMIT),
        name="final",
    )(*args)


def _layer(x, mod, g_pre_mix, g_post_mix, g_pre_ffn, g_post_ffn, w_in, w_dw, b_dw, ln_g, ln_b, w_conv_out,
           w_pool, b_pool, pool_scale, w_out, w_router, router_bias, w_gate_up, w_down, ws_gate_up, ws_down):
    B, S, D = x.shape
    T = B * S
    E = w_router.shape[1]
    mod6 = mod.reshape(B, 6, D)
    mod_m, mod_f = mod6[:, 0:3], mod6[:, 3:6]
    x1 = _mixer(x, mod_m, g_pre_mix, g_post_mix, w_in, w_dw, b_dw, ln_g, ln_b, w_conv_out, w_pool, b_pool,
                pool_scale, w_out, ts=min(512, S))
    h2p, eidx, rank, wcol, cnt = _route(x1, mod_f[:, 0:2], g_pre_ffn, w_router, router_bias, tt=min(512, S))
    TM = 512
    NT = (T * TOP_K) // TM + E
    dest, tstart = _plan(cnt, eidx, rank, TM=TM)
    xs = _dispatch(h2p, dest, NT * TM)
    ys = _experts(xs, tstart, w_gate_up, w_down, TM=TM)
    x1f = x1.reshape(T, D)
    wsgu, wsd = ws_gate_up.astype(bf16), ws_down.astype(bf16)
    tt = min(512, S)
    n_chunks = N_TAIL_CHUNKS
    while (T // n_chunks) % tt:
        n_chunks //= 2
    tc = T // n_chunks
    sizes = [tc] * n_chunks
    if (tc // 2) % tt == 0:
        sizes = [tc // 2, tc // 2] + sizes[1:]
    out, t0 = None, 0
    for size in sizes:
        yg = _collect(ys, dest, t0, size)
        out = _final(yg, wcol, h2p, x1f, mod_f, g_post_ffn, wsgu, wsd, out, t0=t0, tt=tt, S=S)
        t0 += size
    return out.reshape(B, S, D)


def kernel(x, c, w_ada, b_ada, g_pre_mix, g_post_mix, g_pre_ffn, g_post_ffn, w_in, w_dw, b_dw, ln_g, ln_b,
           w_conv_out, w_pool, b_pool, pool_scale, w_out, w_router, router_bias, w_gate_up, w_down, ws_gate_up,
           ws_down):
    depth = w_ada.shape[0]
    for l in range(depth):
        mod = _adaln(c, w_ada[l], b_ada[l])
        x = _layer(x, mod, g_pre_mix[l], g_post_mix[l], g_pre_ffn[l], g_post_ffn[l], w_in[l], w_dw[l], b_dw[l],
                   ln_g[l], ln_b[l], w_conv_out[l], w_pool[l], b_pool[l], pool_scale[l], w_out[l], w_router[l],
                   router_bias[l], w_gate_up[l], w_down[l], ws_gate_up[l], ws_down[l])
    return x
```
